```python
import jax, jax.numpy as jnp
from jax import lax
import numpy as np

D_MODEL = 1024
BATCH = 2
SEQ = 16384
DEPTH = 1

MLA_HEADS = 8
QK_NOPE = 64
QK_ROPE = 32
V_HEAD = 64
Q_LORA = 384
KV_LORA = 256
ROPE_THETA = 10000.0
Q_BLOCK = 128
LRU_WIDTH = D_MODEL
LRU_BLOCKS = 8
LRU_BLOCK = LRU_WIDTH // LRU_BLOCKS
CONV_WIDTH = 4
LRU_C = 8.0
D_FF = 2816
FFN_RES_WEIGHT = 0.5
NORM_EPS = 1e-6
IN_SPLITS = (Q_LORA, KV_LORA, QK_ROPE, LRU_WIDTH, LRU_WIDTH, D_MODEL, D_MODEL)
D_IN = Q_LORA + KV_LORA + QK_ROPE + 2 * LRU_WIDTH + 2 * D_MODEL

kernel_name = 'hybrid_mla_rglru_macaron_sandwich'


def _rmsnorm(x, g):
    xf = x.astype(jnp.float32)
    y = xf * lax.rsqrt(jnp.mean(xf * xf, axis=-1, keepdims=True) + NORM_EPS)
    return (y * g.astype(jnp.float32)).astype(x.dtype)


def _swiglu(x, w_gate, w_up, w_down):
    return (jax.nn.silu(x @ w_gate) * (x @ w_up)) @ w_down


def _half_ffn(h, pre_g, w_gate, w_up, w_down, post_g):
    y = _swiglu(_rmsnorm(h, pre_g), w_gate, w_up, w_down)
    return h + FFN_RES_WEIGHT * _rmsnorm(y, post_g)


def _rope_tables(positions):
    half = QK_ROPE // 2
    inv_freq = ROPE_THETA ** (-jnp.arange(half, dtype=jnp.float32) / half)
    ang = positions.astype(jnp.float32)[..., None] * inv_freq
    return jnp.cos(ang), jnp.sin(ang)


def _apply_rope(x, cos, sin):
    half = x.shape[-1] // 2
    xf = x.astype(jnp.float32)
    x1, x2 = xf[..., :half], xf[..., half:]
    return jnp.concatenate([x1 * cos - x2 * sin, x2 * cos + x1 * sin], axis=-1).astype(x.dtype)


def _mla_branch(c_q, c_kv, k_rope_raw, positions, q_norm_g, w_uq, kv_norm_g, w_ukv, w_o_mla):
    b, s = c_q.shape[0], c_q.shape[1]
    q = (_rmsnorm(c_q, q_norm_g) @ w_uq).reshape(b, s, MLA_HEADS, QK_NOPE + QK_ROPE)
    kv = (_rmsnorm(c_kv, kv_norm_g) @ w_ukv).reshape(b, s, MLA_HEADS, QK_NOPE + V_HEAD)
    q_nope, q_rope = q[..., :QK_NOPE], q[..., QK_NOPE:]
    k_nope, v = kv[..., :QK_NOPE], kv[..., QK_NOPE:]
    cos, sin = _rope_tables(positions)
    q_rope = _apply_rope(q_rope, cos[:, :, None, :], sin[:, :, None, :])
    k_rope = _apply_rope(k_rope_raw, cos, sin)
    scale = (QK_NOPE + QK_ROPE) ** -0.5
    k_idx = jnp.arange(s)

    def attend(i):
        start = i * Q_BLOCK
        qn = lax.dynamic_slice_in_dim(q_nope, start, Q_BLOCK, axis=1)
        qr = lax.dynamic_slice_in_dim(q_rope, start, Q_BLOCK, axis=1)
        sc = jnp.einsum('bqhd,bkhd->bhqk', qn, k_nope) + jnp.einsum('bqhr,bkr->bhqk', qr, k_rope)
        sc = sc.astype(jnp.float32) * scale
        causal = k_idx[None, :] <= (start + jnp.arange(Q_BLOCK))[:, None]
        sc = jnp.where(causal[None, None], sc, -jnp.inf)
        p = jax.nn.softmax(sc, axis=-1).astype(v.dtype)
        return jnp.einsum('bhqk,bkhd->bqhd', p, v)

    o = lax.map(attend, jnp.arange(s // Q_BLOCK))
    o = jnp.moveaxis(o, 0, 1).reshape(b, s, MLA_HEADS * V_HEAD)
    return o @ w_o_mla


def _lru_combine(left, right):
    a_l, b_l = left
    a_r, b_r = right
    return a_l * a_r, a_r * b_l + b_r


def _rglru_branch(xb, yb, conv_w, conv_b, w_rg, b_rg, w_ig, b_ig, lru_lambda, w_o_lru):
    b, s, w = xb.shape
    xpad = jnp.pad(xb, ((0, 0), (CONV_WIDTH - 1, 0), (0, 0)))
    xc = conv_b + sum(xpad[:, k:k + s] * conv_w[k] for k in range(CONV_WIDTH))
    xg = xc.reshape(b, s, LRU_BLOCKS, LRU_BLOCK)
    r = jax.nn.sigmoid(jnp.einsum('bsgi,gij->bsgj', xg, w_rg) + b_rg).reshape(b, s, w)
    ig = jax.nn.sigmoid(jnp.einsum('bsgi,gij->bsgj', xg, w_ig) + b_ig).reshape(b, s, w)
    log_a = -LRU_C * r.astype(jnp.float32) * jax.nn.softplus(-lru_lambda.astype(jnp.float32))
    a = jnp.exp(log_a)
    gated_x = jnp.sqrt(-jnp.expm1(2.0 * log_a)) * (ig * xc).astype(jnp.float32)
    _, h = lax.associative_scan(_lru_combine, (a, gated_x), axis=1)
    return (h.astype(xb.dtype) * jax.nn.gelu(yb)) @ w_o_lru


def setup_inputs(seed: int = 0) -> dict:
    key = jax.random.key(seed)
    ks = jax.random.split(key, 32)
    L = DEPTH

    def dense(k, fan_in, *shape):
        return jax.random.normal(k, (L,) + shape, jnp.float32) * fan_in ** -0.5

    def gain(k, dim):
        return 1.0 + 0.02 * jax.random.normal(k, (L, dim), jnp.float32)

    def bias(k, *shape):
        return 0.01 * jax.random.normal(k, (L,) + shape, jnp.float32)

    x = jax.random.normal(ks[0], (BATCH, SEQ, D_MODEL), jnp.float32)
    positions = jnp.broadcast_to(jnp.arange(SEQ, dtype=jnp.int32), (BATCH, SEQ))
    a_c = jax.random.uniform(ks[20], (L, LRU_WIDTH), jnp.float32, 0.9, 0.999)
    a0 = a_c ** (1.0 / LRU_C)
    lru_lambda = jnp.log(a0) - jnp.log1p(-a0)
    return {
        'x': x,
        'positions': positions,
        'ffn1_pre_g': gain(ks[1], D_MODEL),
        'ffn1_w_gate': dense(ks[2], D_MODEL, D_MODEL, D_FF),
        'ffn1_w_up': dense(ks[3], D_MODEL, D_MODEL, D_FF),
        'ffn1_w_down': dense(ks[4], D_FF, D_FF, D_MODEL),
        'ffn1_post_g': gain(ks[5], D_MODEL),
        'mix_pre_g': gain(ks[6], D_MODEL),
        'w_in': dense(ks[7], D_MODEL, D_MODEL, D_IN),
        'q_norm_g': gain(ks[8], Q_LORA),
        'w_uq': dense(ks[9], Q_LORA, Q_LORA, MLA_HEADS * (QK_NOPE + QK_ROPE)),
        'kv_norm_g': gain(ks[10], KV_LORA),
        'w_ukv': dense(ks[11], KV_LORA, KV_LORA, MLA_HEADS * (QK_NOPE + V_HEAD)),
        'w_o_mla': dense(ks[12], MLA_HEADS * V_HEAD, MLA_HEADS * V_HEAD, D_MODEL),
        'conv_w': dense(ks[13], CONV_WIDTH, CONV_WIDTH, LRU_WIDTH),
        'conv_b': bias(ks[14], LRU_WIDTH),
        'w_rg': dense(ks[15], LRU_BLOCK, LRU_BLOCKS, LRU_BLOCK, LRU_BLOCK),
        'b_rg': bias(ks[16], LRU_BLOCKS, LRU_BLOCK),
        'w_ig': dense(ks[17], LRU_BLOCK, LRU_BLOCKS, LRU_BLOCK, LRU_BLOCK),
        'b_ig': bias(ks[18], LRU_BLOCKS, LRU_BLOCK),
        'lru_lambda': lru_lambda,
        'w_o_lru': dense(ks[19], LRU_WIDTH, LRU_WIDTH, D_MODEL),
        'w_out': dense(ks[21], D_MODEL, D_MODEL, D_MODEL),
        'mix_post_g': gain(ks[22], D_MODEL),
        'ffn2_pre_g': gain(ks[23], D_MODEL),
        'ffn2_w_gate': dense(ks[24], D_MODEL, D_MODEL, D_FF),
        'ffn2_w_up': dense(ks[25], D_MODEL, D_MODEL, D_FF),
        'ffn2_w_down': dense(ks[26], D_FF, D_FF, D_MODEL),
        'ffn2_post_g': gain(ks[27], D_MODEL),
    }


def reference(x, positions, ffn1_pre_g, ffn1_w_gate, ffn1_w_up, ffn1_w_down, ffn1_post_g,
              mix_pre_g, w_in, q_norm_g, w_uq, kv_norm_g, w_ukv, w_o_mla,
              conv_w, conv_b, w_rg, b_rg, w_ig, b_ig, lru_lambda, w_o_lru, w_out, mix_post_g,
              ffn2_pre_g, ffn2_w_gate, ffn2_w_up, ffn2_w_down, ffn2_post_g):
    split_points = np.cumsum(IN_SPLITS)[:-1].tolist()
    h = x
    for l in range(DEPTH):
        h = _half_ffn(h, ffn1_pre_g[l], ffn1_w_gate[l], ffn1_w_up[l], ffn1_w_down[l], ffn1_post_g[l])
        u = _rmsnorm(h, mix_pre_g[l])
        c_q, c_kv, k_rope_raw, xb, yb, gate_mla, gate_lru = jnp.split(u @ w_in[l], split_points, axis=-1)
        o_mla = _mla_branch(c_q, c_kv, k_rope_raw, positions, q_norm_g[l], w_uq[l],
                            kv_norm_g[l], w_ukv[l], w_o_mla[l])
        o_lru = _rglru_branch(xb, yb, conv_w[l], conv_b[l], w_rg[l], b_rg[l], w_ig[l], b_ig[l],
                              lru_lambda[l], w_o_lru[l])
        merged = jax.nn.sigmoid(gate_mla) * o_mla + jax.nn.sigmoid(gate_lru) * o_lru
        h = h + _rmsnorm(merged @ w_out[l], mix_post_g[l])
        h = _half_ffn(h, ffn2_pre_g[l], ffn2_w_gate[l], ffn2_w_up[l], ffn2_w_down[l], ffn2_post_g[l])
    return h
```

```python
import functools

import jax
import jax.numpy as jnp
from jax import lax
from jax.experimental import pallas as pl
from jax.experimental.pallas import tpu as pltpu

MLA_HEADS = 8
QK_NOPE = 64
QK_ROPE = 32
V_HEAD = 64
Q_LORA = 384
KV_LORA = 256
ROPE_THETA = 10000.0
LRU_BLOCKS = 8
CONV_WIDTH = 4
LRU_C = 8.0
FFN_RES_WEIGHT = 0.5
NORM_EPS = 1e-6

LANES = 128
SUBLANES = 8
HEAD_PAD = LANES
VMEM_LIMIT = 56 * 1024 * 1024

F32 = jnp.float32
BF16 = jnp.bfloat16


def _rms(x, g):
    return x * lax.rsqrt(jnp.mean(x * x, axis=-1, keepdims=True) + NORM_EPS) * g


def _dot(a, b):
    return jnp.dot(a, b, preferred_element_type=F32)


def _resident(shape):
    nd = len(shape)
    return pl.BlockSpec(shape, lambda *_: (0,) * nd, pipeline_mode=pl.Buffered(1))


def _ffn_kernel(h_ref, pre_g_ref, wg_ref, wu_ref, wd_ref, post_g_ref, o_ref, act_ref, *, ff_chunk):
    x = h_ref[...]
    xn = _rms(x, pre_g_ref[...]).astype(BF16)
    d_ff = wg_ref.shape[1]
    for c in range(d_ff // ff_chunk):
        sl = slice(c * ff_chunk, (c + 1) * ff_chunk)
        g = _dot(xn, wg_ref[:, sl])
        u = _dot(xn, wu_ref[:, sl])
        act_ref[:, sl] = (g * jax.nn.sigmoid(g) * u).astype(BF16)
    y = _dot(act_ref[...], wd_ref[...])
    o_ref[...] = x + FFN_RES_WEIGHT * _rms(y, post_g_ref[...])


def _ffn(h, pre_g, wg, wu, wd, post_g, *, tm, ff_chunk):
    t, d = h.shape
    d_ff = wg.shape[1]
    return pl.pallas_call(
        functools.partial(_ffn_kernel, ff_chunk=ff_chunk),
        grid=(t // tm,),
        in_specs=[
            pl.BlockSpec((tm, d), lambda i: (i, 0)),
            _resident((1, d)),
            _resident((d, d_ff)),
            _resident((d, d_ff)),
            _resident((d_ff, d)),
            _resident((1, d)),
        ],
        out_specs=pl.BlockSpec((tm, d), lambda i: (i, 0)),
        out_shape=jax.ShapeDtypeStruct((t, d), F32),
        scratch_shapes=[pltpu.VMEM((tm, d_ff), BF16)],
        compiler_params=pltpu.CompilerParams(
            dimension_semantics=("arbitrary",), vmem_limit_bytes=VMEM_LIMIT),
        name="ffn",
    )(h, pre_g, wg, wu, wd, post_g)


def _inproj_kernel(h_ref, pos_ref, g_ref, wlat_ref, wrest_ref, qg_ref, wuq_ref, kvg_ref, wkv_ref,
                   rope_ref, q_ref, k_ref, v_ref, z_ref, *, scale):
    u = _rms(h_ref[0], g_ref[...]).astype(BF16)
    z_ref[0] = _dot(u, wrest_ref[...])
    lat = _dot(u, wlat_ref[...])
    qn = _rms(lat[:, :Q_LORA], qg_ref[...]).astype(BF16)
    kvn = _rms(lat[:, Q_LORA:Q_LORA + KV_LORA], kvg_ref[...]).astype(BF16)
    k_rope = lat[:, Q_LORA + KV_LORA:]
    q = _dot(qn, wuq_ref[...])
    kv = _dot(kvn, wkv_ref[...])

    ang = pos_ref[0].astype(F32) * rope_ref[0:1, :]
    cos = jnp.cos(ang)
    sin = jnp.sin(ang)
    sin_lo = sin * rope_ref[1:2, :]
    sin_hi = sin * rope_ref[2:3, :]
    half = QK_ROPE // 2

    def rope(r):
        return r * cos + pltpu.roll(r, HEAD_PAD - half, 1) * sin_lo + pltpu.roll(r, half, 1) * sin_hi

    k_rope = rope(k_rope)
    n_heads = q.shape[1] // HEAD_PAD
    for h in range(n_heads):
        sl = slice(h * HEAD_PAD, (h + 1) * HEAD_PAD)
        q_ref[0, :, sl] = (rope(q[:, sl]) * scale).astype(BF16)
        k_ref[0, :, sl] = (kv[:, sl] + k_rope).astype(BF16)
    v_ref[0] = kv[:, n_heads * HEAD_PAD:].astype(BF16)


def _inproj(h, pos, g, wlat, wrest, qg, wuq, kvg, wkv, rope_consts, *, tm, scale):
    b, s, d = h.shape
    hp = wuq.shape[1]
    n_rest = wrest.shape[1]
    tok = lambda w: pl.BlockSpec((1, tm, w), lambda bi, i: (bi, i, 0))
    return pl.pallas_call(
        functools.partial(_inproj_kernel, scale=scale),
        grid=(b, s // tm),
        in_specs=[
            tok(d), tok(1), _resident(g.shape), _resident(wlat.shape), _resident(wrest.shape),
            _resident(qg.shape), _resident(wuq.shape), _resident(kvg.shape), _resident(wkv.shape),
            _resident(rope_consts.shape),
        ],
        out_specs=[tok(hp), tok(hp), tok(hp), tok(n_rest)],
        out_shape=[
            jax.ShapeDtypeStruct((b, s, hp), BF16),
            jax.ShapeDtypeStruct((b, s, hp), BF16),
            jax.ShapeDtypeStruct((b, s, hp), BF16),
            jax.ShapeDtypeStruct((b, s, n_rest), F32),
        ],
        compiler_params=pltpu.CompilerParams(
            dimension_semantics=("arbitrary", "arbitrary"), vmem_limit_bytes=VMEM_LIMIT),
        name="inproj",
    )(h, pos, g, wlat, wrest, qg, wuq, kvg, wkv, rope_consts)


def _attn_kernel(q_ref, k_ref, v_ref, o_ref, *, tq):
    qi = pl.program_id(2)
    q = q_ref[0]

    def block(j, carry, diagonal):
        m, l, acc = carry
        start = pl.multiple_of(j * tq, tq)
        kb = k_ref[0, pl.ds(start, tq), :]
        vb = v_ref[0, pl.ds(start, tq), :]
        s = lax.dot_general(q, kb, (((1,), (1,)), ((), ())), preferred_element_type=F32)
        if diagonal:
            row = lax.broadcasted_iota(jnp.int32, s.shape, 0)
            col = lax.broadcasted_iota(jnp.int32, s.shape, 1)
            s = jnp.where(col <= row, s, -jnp.inf)
        m_new = jnp.maximum(m, jnp.max(s, axis=-1, keepdims=True))
        alpha = jnp.exp(m - m_new)
        p = jnp.exp(s - m_new)
        l = alpha * l + jnp.sum(p, axis=-1, keepdims=True)
        acc = alpha * acc + _dot(p.astype(BF16), vb)
        return m_new, l, acc

    init = (jnp.full((tq, 1), -jnp.inf, F32), jnp.zeros((tq, 1), F32), jnp.zeros((tq, HEAD_PAD), F32))
    carry = lax.fori_loop(0, qi, lambda j, c: block(j, c, False), init)
    _, l, acc = block(qi, carry, True)
    o_ref[0] = (acc / l).astype(BF16)


def _attn(q, k, v, *, tq):
    b, s, hp = q.shape
    n_heads = hp // HEAD_PAD
    qspec = pl.BlockSpec((1, tq, HEAD_PAD), lambda bi, h, i: (bi, i, h))
    kvspec = pl.BlockSpec((1, s, HEAD_PAD), lambda bi, h, i: (bi, 0, h))
    return pl.pallas_call(
        functools.partial(_attn_kernel, tq=tq),
        grid=(b, n_heads, s // tq),
        in_specs=[qspec, kvspec, kvspec],
        out_specs=qspec,
        out_shape=jax.ShapeDtypeStruct((b, s, hp), BF16),
        compiler_params=pltpu.CompilerParams(
            dimension_semantics=("arbitrary", "arbitrary", "arbitrary"), vmem_limit_bytes=VMEM_LIMIT),
        name="attn",
    )(q, k, v)


def _lru_kernel(xb_ref, yb_ref, cw_ref, cb_ref, wrg_ref, brg_ref, wig_ref, big_ref, lam_ref, o_ref,
                xbuf, a_ref, b_ref, hcar, *, tc):
    t = pl.program_id(1)
    halo = SUBLANES

    @pl.when(t == 0)
    def _():
        xbuf[0:halo, :] = jnp.zeros((halo, xbuf.shape[1]), F32)
        hcar[...] = jnp.zeros(hcar.shape, F32)

    xbuf[halo:halo + tc, :] = xb_ref[0]
    xc = cb_ref[...]
    for kk in range(CONV_WIDTH):
        xc = xc + xbuf[pl.ds(halo - (CONV_WIDTH - 1) + kk, tc), :] * cw_ref[kk:kk + 1, :]
    xbuf[0:halo, :] = xbuf[tc:tc + halo, :]

    sp = jax.nn.softplus(-lam_ref[...])
    width = xc.shape[1]
    blk = width // LRU_BLOCKS
    for g in range(LRU_BLOCKS):
        sl = slice(g * blk, (g + 1) * blk)
        xg = xc[:, sl]
        xg16 = xg.astype(BF16)
        r = jax.nn.sigmoid(_dot(xg16, wrg_ref[g]) + brg_ref[:, sl])
        ig = jax.nn.sigmoid(_dot(xg16, wig_ref[g]) + big_ref[:, sl])
        log_a = -LRU_C * r * sp[:, sl]
        a = jnp.exp(log_a)
        a_ref[:, sl] = a
        b_ref[:, sl] = jnp.sqrt(1.0 - a * a) * (ig * xg)

    row = lax.broadcasted_iota(jnp.int32, (SUBLANES, width), 0)

    def group(i, h_prev):
        r0 = pl.multiple_of(i * SUBLANES, SUBLANES)
        a = a_ref[pl.ds(r0, SUBLANES), :]
        bb = b_ref[pl.ds(r0, SUBLANES), :]
        for sh in (1, 2, 4):
            a_sh = jnp.where(row >= sh, pltpu.roll(a, sh, 0), 1.0)
            b_sh = jnp.where(row >= sh, pltpu.roll(bb, sh, 0), 0.0)
            bb = a * b_sh + bb
            a = a * a_sh
        hh = a * h_prev + bb
        b_ref[pl.ds(r0, SUBLANES), :] = hh
        return jnp.broadcast_to(hh[SUBLANES - 1:SUBLANES, :], hh.shape)

    hcar[...] = lax.fori_loop(0, tc // SUBLANES, group, hcar[...])
    o_ref[0] = (b_ref[...] * jax.nn.gelu(yb_ref[0])).astype(BF16)


def _lru(z, cw, cb, wrg, brg, wig, big, lam, *, tc):
    b, s, _ = z.shape
    w = cw.shape[1]
    return pl.pallas_call(
        functools.partial(_lru_kernel, tc=tc),
        grid=(b, s // tc),
        in_specs=[
            pl.BlockSpec((1, tc, w), lambda bi, i: (bi, i, 0)),
            pl.BlockSpec((1, tc, w), lambda bi, i: (bi, i, 1)),
            _resident(cw.shape), _resident(cb.shape), _resident(wrg.shape), _resident(brg.shape),
            _resident(wig.shape), _resident(big.shape), _resident(lam.shape),
        ],
        out_specs=pl.BlockSpec((1, tc, w), lambda bi, i: (bi, i, 0)),
        out_shape=jax.ShapeDtypeStruct((b, s, w), BF16),
        scratch_shapes=[
            pltpu.VMEM((tc + SUBLANES, w), F32),
            pltpu.VMEM((tc, w), F32),
            pltpu.VMEM((tc, w), F32),
            pltpu.VMEM((SUBLANES, w), F32),
        ],
        compiler_params=pltpu.CompilerParams(
            dimension_semantics=("arbitrary", "arbitrary"), vmem_limit_bytes=VMEM_LIMIT),
        name="lru",
    )(z, z, cw, cb, wrg, brg, wig, big, lam)


def _merge_kernel(o_ref, lg_ref, ga_ref, gb_ref, h_ref, womla_ref, wolru_ref, wout_ref, g_ref, out_ref):
    o_mla = _dot(o_ref[0], womla_ref[...])
    o_lru = _dot(lg_ref[0], wolru_ref[...])
    merged = jax.nn.sigmoid(ga_ref[0]) * o_mla + jax.nn.sigmoid(gb_ref[0]) * o_lru
    y = _dot(merged.astype(BF16), wout_ref[...])
    out_ref[0] = h_ref[0] + _rms(y, g_ref[...])


def _merge(o, lg, z, h, womla, wolru, wout, g, *, tm):
    b, s, d = h.shape
    tok = lambda col: pl.BlockSpec((1, tm, d), lambda bi, i: (bi, i, col))
    return pl.pallas_call(
        _merge_kernel,
        grid=(b, s // tm),
        in_specs=[
            tok(0), tok(0), tok(2), tok(3), tok(0),
            _resident(womla.shape), _resident(wolru.shape), _resident(wout.shape), _resident(g.shape),
        ],
        out_specs=tok(0),
        out_shape=jax.ShapeDtypeStruct((b, s, d), F32),
        compiler_params=pltpu.CompilerParams(
            dimension_semantics=("arbitrary", "arbitrary"), vmem_limit_bytes=VMEM_LIMIT),
        name="merge",
    )(o, lg, z, z, h, womla, wolru, wout, g)


def _pad_heads(w, per_head):
    k = w.shape[0]
    w = w.reshape(k, MLA_HEADS, per_head)
    return jnp.pad(w, ((0, 0), (0, 0), (0, HEAD_PAD - per_head))).reshape(k, MLA_HEADS * HEAD_PAD)


def _rope_consts():
    half = QK_ROPE // 2
    inv_freq = ROPE_THETA ** (-jnp.arange(half, dtype=F32) / half)
    lane = jnp.arange(HEAD_PAD)
    lo = (lane >= QK_NOPE) & (lane < QK_NOPE + half)
    hi = (lane >= QK_NOPE + half) & (lane < QK_NOPE + QK_ROPE)
    freq = jnp.where(lo | hi, inv_freq[(lane - QK_NOPE) % half], 0.0)
    rows = jnp.stack([freq, jnp.where(lo, -1.0, 0.0), jnp.where(hi, 1.0, 0.0)]).astype(F32)
    return jnp.pad(rows, ((0, SUBLANES - rows.shape[0]), (0, 0)))


def kernel(x, positions, ffn1_pre_g, ffn1_w_gate, ffn1_w_up, ffn1_w_down, ffn1_post_g, mix_pre_g, w_in, q_norm_g, w_uq, kv_norm_g, w_ukv, w_o_mla, conv_w, conv_b, w_rg, b_rg, w_ig, b_ig, lru_lambda, w_o_lru, w_out, mix_post_g, ffn2_pre_g, ffn2_w_gate, ffn2_w_up, ffn2_w_down, ffn2_post_g):
    b, s, d = x.shape
    depth = w_in.shape[0]
    tm = min(512, s)
    tq = min(512, s)
    tc = min(256, s)
    ff_chunk = ffn1_w_gate.shape[2] // 2
    scale = (QK_NOPE + QK_ROPE) ** -0.5
    n_lat = Q_LORA + KV_LORA
    rope_consts = _rope_consts()
    pos = positions.reshape(b, s, 1)
    row = lambda v: v.reshape(1, -1)

    h = x.reshape(b * s, d)
    for l in range(depth):
        h = _ffn(h, row(ffn1_pre_g[l]), ffn1_w_gate[l].astype(BF16), ffn1_w_up[l].astype(BF16),
                 ffn1_w_down[l].astype(BF16), row(ffn1_post_g[l]), tm=tm, ff_chunk=ff_chunk)

        wi = w_in[l]
        w_krope = jnp.pad(wi[:, n_lat:n_lat + QK_ROPE], ((0, 0), (QK_NOPE, HEAD_PAD - QK_NOPE - QK_ROPE)))
        wlat = jnp.concatenate([wi[:, :n_lat], w_krope], axis=1).astype(BF16)
        wrest = wi[:, n_lat + QK_ROPE:].astype(BF16)
        wuq = _pad_heads(w_uq[l], QK_NOPE + QK_ROPE).astype(BF16)
        wkv = w_ukv[l].reshape(KV_LORA, MLA_HEADS, QK_NOPE + V_HEAD)
        wkv = jnp.concatenate([_pad_heads(wkv[:, :, :QK_NOPE].reshape(KV_LORA, -1), QK_NOPE),
                               _pad_heads(wkv[:, :, QK_NOPE:].reshape(KV_LORA, -1), V_HEAD)], axis=1).astype(BF16)
        h3 = h.reshape(b, s, d)
        q, k, v, z = _inproj(h3, pos, row(mix_pre_g[l]), wlat, wrest, row(q_norm_g[l]), wuq,
                             row(kv_norm_g[l]), wkv, rope_consts, tm=tm, scale=scale)

        o = _attn(q, k, v, tq=tq)
        lg = _lru(z, conv_w[l], row(conv_b[l]), w_rg[l].astype(BF16), row(b_rg[l]),
                  w_ig[l].astype(BF16), row(b_ig[l]), row(lru_lambda[l]), tc=tc)

        womla = w_o_mla[l].reshape(MLA_HEADS, V_HEAD, d)
        womla = jnp.pad(womla, ((0, 0), (0, HEAD_PAD - V_HEAD), (0, 0))).reshape(MLA_HEADS * HEAD_PAD, d)
        h = _merge(o, lg, z, h3, womla.astype(BF16), w_o_lru[l].astype(BF16), w_out[l].astype(BF16),
                   row(mix_post_g[l]), tm=tm).reshape(b * s, d)

        h = _ffn(h, row(ffn2_pre_g[l]), ffn2_w_gate[l].astype(BF16), ffn2_w_up[l].astype(BF16),
                 ffn2_w_down[l].astype(BF16), row(ffn2_post_g[l]), tm=tm, ff_chunk=ff_chunk)
    return h.reshape(b, s, d)
```

```python
import functools
import math

import jax
import jax.numpy as jnp
from jax import lax
from jax.experimental import pallas as pl
from jax.experimental.pallas import tpu as pltpu

MLA_HEADS = 8
QK_NOPE = 64
QK_ROPE = 32
V_HEAD = 64
Q_LORA = 384
KV_LORA = 256
ROPE_THETA = 10000.0
LRU_BLOCKS = 8
CONV_WIDTH = 4
LRU_C = 8.0
FFN_RES_WEIGHT = 0.5
NORM_EPS = 1e-6

LANES = 128
SUBLANES = 8
BF16_ROWS = 16
HEAD_PAD = LANES
V_ROWS = V_HEAD + BF16_ROWS
ATTN_TK = 256
VMEM_LIMIT = 56 * 1024 * 1024

F32 = jnp.float32
BF16 = jnp.bfloat16
NT_DIMS = (((1,), (1,)), ((), ()))
TN_DIMS = (((0,), (0,)), ((), ()))


def _rms(x, g):
    return x * lax.rsqrt(jnp.mean(x * x, axis=-1, keepdims=True) + NORM_EPS) * g


def _dot(a, b):
    return jnp.dot(a, b, preferred_element_type=F32)


def _resident(shape):
    nd = len(shape)
    return pl.BlockSpec(shape, lambda *_: (0,) * nd, pipeline_mode=pl.Buffered(1))


def _ffn_kernel(h_ref, pre_g_ref, wg_ref, wu_ref, wd_ref, post_g_ref, o_ref, act_ref, *, ff_chunk):
    x = h_ref[...]
    xn = _rms(x, pre_g_ref[...]).astype(BF16)
    d_ff = wg_ref.shape[1]
    for c in range(d_ff // ff_chunk):
        sl = slice(c * ff_chunk, (c + 1) * ff_chunk)
        g = _dot(xn, wg_ref[:, sl])
        u = _dot(xn, wu_ref[:, sl])
        act_ref[:, sl] = (g * jax.nn.sigmoid(g) * u).astype(BF16)
    y = _dot(act_ref[...], wd_ref[...])
    o_ref[...] = x + FFN_RES_WEIGHT * _rms(y, post_g_ref[...])


def _ffn(h, pre_g, wg, wu, wd, post_g, *, tm, ff_chunk):
    t, d = h.shape
    d_ff = wg.shape[1]
    return pl.pallas_call(
        functools.partial(_ffn_kernel, ff_chunk=ff_chunk),
        grid=(t // tm,),
        in_specs=[
            pl.BlockSpec((tm, d), lambda i: (i, 0)),
            _resident((1, d)),
            _resident((d, d_ff)),
            _resident((d, d_ff)),
            _resident((d_ff, d)),
            _resident((1, d)),
        ],
        out_specs=pl.BlockSpec((tm, d), lambda i: (i, 0)),
        out_shape=jax.ShapeDtypeStruct((t, d), F32),
        scratch_shapes=[pltpu.VMEM((tm, d_ff), BF16)],
        compiler_params=pltpu.CompilerParams(
            dimension_semantics=("arbitrary",), vmem_limit_bytes=VMEM_LIMIT),
        name="ffn",
    )(h, pre_g, wg, wu, wd, post_g)


def _inproj_kernel(h_ref, pos_ref, g_ref, wlat_ref, wrest_ref, qg_ref, wuq_ref, kvg_ref, wk_ref,
                   wvt_ref, vones_ref, rope_ref, q_ref, k_ref, vt_ref, z_ref, *, q_scale):
    u = _rms(h_ref[0], g_ref[...]).astype(BF16)
    z_ref[0] = _dot(u, wrest_ref[...])
    lat = _dot(u, wlat_ref[...])
    qn = _rms(lat[:, :Q_LORA], qg_ref[...]).astype(BF16)
    kvn = _rms(lat[:, Q_LORA:Q_LORA + KV_LORA], kvg_ref[...]).astype(BF16)
    k_rope = lat[:, Q_LORA + KV_LORA:]
    q = _dot(qn, wuq_ref[...])
    k = _dot(kvn, wk_ref[...])
    vt = lax.dot_general(wvt_ref[...], kvn, NT_DIMS, preferred_element_type=F32) + vones_ref[...]

    ang = pos_ref[0].astype(F32) * rope_ref[0:1, :]
    cos = jnp.cos(ang)
    sin = jnp.sin(ang)
    sin_lo = sin * rope_ref[1:2, :]
    sin_hi = sin * rope_ref[2:3, :]
    half = QK_ROPE // 2

    def rope(r):
        return r * cos + pltpu.roll(r, HEAD_PAD - half, 1) * sin_lo + pltpu.roll(r, half, 1) * sin_hi

    k_rope = rope(k_rope)
    for h in range(q.shape[1] // HEAD_PAD):
        sl = slice(h * HEAD_PAD, (h + 1) * HEAD_PAD)
        q_ref[0, :, sl] = (rope(q[:, sl]) * q_scale).astype(BF16)
        k_ref[0, :, sl] = (k[:, sl] + k_rope).astype(BF16)
    for c in range(vt_ref.shape[1]):
        vt_ref[0, c] = vt[:, c * ATTN_TK:(c + 1) * ATTN_TK].astype(BF16)


def _inproj(h, pos, g, wlat, wrest, qg, wuq, kvg, wk, wvt, vones, rope_consts, *, tm, q_scale):
    b, s, d = h.shape
    hp = wuq.shape[1]
    n_rest = wrest.shape[1]
    vrows = wvt.shape[0]
    tok = lambda w: pl.BlockSpec((1, tm, w), lambda bi, i: (bi, i, 0))
    consts = (g, wlat, wrest, qg, wuq, kvg, wk, wvt, vones, rope_consts)
    return pl.pallas_call(
        functools.partial(_inproj_kernel, q_scale=q_scale),
        grid=(b, s // tm),
        in_specs=[tok(d), tok(1)] + [_resident(c.shape) for c in consts],
        out_specs=[tok(hp), tok(hp),
                   pl.BlockSpec((1, tm // ATTN_TK, vrows, ATTN_TK), lambda bi, i: (bi, i, 0, 0)),
                   tok(n_rest)],
        out_shape=[
            jax.ShapeDtypeStruct((b, s, hp), BF16),
            jax.ShapeDtypeStruct((b, s, hp), BF16),
            jax.ShapeDtypeStruct((b, s // ATTN_TK, vrows, ATTN_TK), BF16),
            jax.ShapeDtypeStruct((b, s, n_rest), F32),
        ],
        compiler_params=pltpu.CompilerParams(
            dimension_semantics=("arbitrary", "arbitrary"), vmem_limit_bytes=VMEM_LIMIT),
        name="inproj",
    )(h, pos, *consts)


def _attn_kernel(q_ref, k_ref, vt_ref, o_ref, sa_ref, sb_ref, *, tq):
    tk = ATTN_TK
    qi = pl.program_id(2)
    q = q_ref[0]
    per_tile = tq // tk

    def scores(blk):
        kb = k_ref[0, pl.ds(pl.multiple_of(blk * tk, tk), tk), :]
        return lax.dot_general(kb, q, NT_DIMS, preferred_element_type=F32)

    def prefetch(blk, s_ref):
        s = scores(blk)
        s_ref[...] = s
        return jnp.max(s, axis=0, keepdims=True)

    def consume(s, blk, bmax, m, acc):
        m_new = jnp.maximum(m, bmax)
        alpha = jnp.exp2(m - m_new)
        p = jnp.exp2((s - m_new).astype(BF16))
        return m_new, alpha * acc + _dot(vt_ref[0, blk], p)

    def pair(u, carry):
        m, acc, bmax_a = carry
        blk = per_tile * u
        bmax_b = prefetch(blk + 1, sb_ref)
        m, acc = consume(sa_ref[...], blk, bmax_a, m, acc)
        bmax_a = prefetch(blk + 2, sa_ref)
        m, acc = consume(sb_ref[...], blk + 1, bmax_b, m, acc)
        return m, acc, bmax_a

    m = jnp.full((1, tq), -jnp.inf, F32)
    acc = jnp.zeros((V_ROWS, tq), F32)
    m, acc, _ = lax.fori_loop(0, qi, pair, (m, acc, prefetch(0, sa_ref)))

    row = lax.broadcasted_iota(jnp.int32, (tk, tq), 0)
    col = lax.broadcasted_iota(jnp.int32, (tk, tq), 1)
    first = per_tile * qi
    for d in range(per_tile):
        s = sa_ref[...] if d == 0 else scores(first + d)
        s = jnp.where(row + d * tk <= col, s, -jnp.inf)
        m, acc = consume(s, first + d, jnp.max(s, axis=0, keepdims=True), m, acc)
    o_ref[0] = (acc[:V_HEAD] / acc[V_HEAD:V_HEAD + 1]).astype(BF16)


def _attn(q, k, vt, *, tq):
    b, s, hp = q.shape
    n_heads = hp // HEAD_PAD
    assert tq == 2 * ATTN_TK, "the key loop consumes blocks in pairs"
    return pl.pallas_call(
        functools.partial(_attn_kernel, tq=tq),
        grid=(b, n_heads, s // tq),
        in_specs=[
            pl.BlockSpec((1, tq, HEAD_PAD), lambda bi, h, i: (bi, i, h)),
            pl.BlockSpec((1, s, HEAD_PAD), lambda bi, h, i: (bi, 0, h)),
            pl.BlockSpec((1, s // ATTN_TK, V_ROWS, ATTN_TK), lambda bi, h, i: (bi, 0, h, 0)),
        ],
        out_specs=pl.BlockSpec((1, V_HEAD, tq), lambda bi, h, i: (bi, h, i)),
        out_shape=jax.ShapeDtypeStruct((b, n_heads * V_HEAD, s), BF16),
        scratch_shapes=[pltpu.VMEM((ATTN_TK, tq), F32), pltpu.VMEM((ATTN_TK, tq), F32)],
        compiler_params=pltpu.CompilerParams(
            dimension_semantics=("arbitrary", "arbitrary", "arbitrary"), vmem_limit_bytes=VMEM_LIMIT),
        name="attn",
    )(q, k, vt)


def _lru_kernel(xb_ref, yb_ref, cw_ref, cb_ref, wrg_ref, brg_ref, wig_ref, big_ref, lam_ref, o_ref,
                xbuf, a_ref, b_ref, hcar, *, tc):
    t = pl.program_id(1)
    halo = SUBLANES

    @pl.when(t == 0)
    def _():
        xbuf[0:halo, :] = jnp.zeros((halo, xbuf.shape[1]), F32)
        hcar[...] = jnp.zeros(hcar.shape, F32)

    xbuf[halo:halo + tc, :] = xb_ref[0]
    xc = cb_ref[...]
    for kk in range(CONV_WIDTH):
        xc = xc + xbuf[pl.ds(halo - (CONV_WIDTH - 1) + kk, tc), :] * cw_ref[kk:kk + 1, :]
    xbuf[0:halo, :] = xbuf[tc:tc + halo, :]

    sp = jax.nn.softplus(-lam_ref[...])
    width = xc.shape[1]
    blk = width // LRU_BLOCKS
    for g in range(LRU_BLOCKS):
        sl = slice(g * blk, (g + 1) * blk)
        xg = xc[:, sl]
        xg16 = xg.astype(BF16)
        r = jax.nn.sigmoid(_dot(xg16, wrg_ref[g]) + brg_ref[:, sl])
        ig = jax.nn.sigmoid(_dot(xg16, wig_ref[g]) + big_ref[:, sl])
        log_a = -LRU_C * r * sp[:, sl]
        a = jnp.exp(log_a)
        a_ref[:, sl] = a
        b_ref[:, sl] = jnp.sqrt(1.0 - a * a) * (ig * xg)

    row = lax.broadcasted_iota(jnp.int32, (SUBLANES, width), 0)

    def group(i, h_prev):
        r0 = pl.multiple_of(i * SUBLANES, SUBLANES)
        a = a_ref[pl.ds(r0, SUBLANES), :]
        bb = b_ref[pl.ds(r0, SUBLANES), :]
        for sh in (1, 2, 4):
            a_sh = jnp.where(row >= sh, pltpu.roll(a, sh, 0), 1.0)
            b_sh = jnp.where(row >= sh, pltpu.roll(bb, sh, 0), 0.0)
            bb = a * b_sh + bb
            a = a * a_sh
        hh = a * h_prev + bb
        b_ref[pl.ds(r0, SUBLANES), :] = hh
        return jnp.broadcast_to(hh[SUBLANES - 1:SUBLANES, :], hh.shape)

    hcar[...] = lax.fori_loop(0, tc // SUBLANES, group, hcar[...])
    o_ref[0] = (b_ref[...] * jax.nn.gelu(yb_ref[0])).astype(BF16)


def _lru(z, cw, cb, wrg, brg, wig, big, lam, *, tc):
    b, s, _ = z.shape
    w = cw.shape[1]
    return pl.pallas_call(
        functools.partial(_lru_kernel, tc=tc),
        grid=(b, s // tc),
        in_specs=[
            pl.BlockSpec((1, tc, w), lambda bi, i: (bi, i, 0)),
            pl.BlockSpec((1, tc, w), lambda bi, i: (bi, i, 1)),
            _resident(cw.shape), _resident(cb.shape), _resident(wrg.shape), _resident(brg.shape),
            _resident(wig.shape), _resident(big.shape), _resident(lam.shape),
        ],
        out_specs=pl.BlockSpec((1, tc, w), lambda bi, i: (bi, i, 0)),
        out_shape=jax.ShapeDtypeStruct((b, s, w), BF16),
        scratch_shapes=[
            pltpu.VMEM((tc + SUBLANES, w), F32),
            pltpu.VMEM((tc, w), F32),
            pltpu.VMEM((tc, w), F32),
            pltpu.VMEM((SUBLANES, w), F32),
        ],
        compiler_params=pltpu.CompilerParams(
            dimension_semantics=("arbitrary", "arbitrary"), vmem_limit_bytes=VMEM_LIMIT),
        name="lru",
    )(z, z, cw, cb, wrg, brg, wig, big, lam)


def _merge_kernel(ot_ref, lg_ref, ga_ref, gb_ref, h_ref, womla_ref, wolru_ref, wout_ref, g_ref, out_ref):
    o_mla = lax.dot_general(ot_ref[0], womla_ref[...], TN_DIMS, preferred_element_type=F32)
    o_lru = _dot(lg_ref[0], wolru_ref[...])
    merged = jax.nn.sigmoid(ga_ref[0]) * o_mla + jax.nn.sigmoid(gb_ref[0]) * o_lru
    y = _dot(merged.astype(BF16), wout_ref[...])
    out_ref[0] = h_ref[0] + _rms(y, g_ref[...])


def _merge(ot, lg, z, h, womla, wolru, wout, g, *, tm):
    b, s, d = h.shape
    tok = lambda col: pl.BlockSpec((1, tm, d), lambda bi, i: (bi, i, col))
    return pl.pallas_call(
        _merge_kernel,
        grid=(b, s // tm),
        in_specs=[
            pl.BlockSpec((1, ot.shape[1], tm), lambda bi, i: (bi, 0, i)),
            tok(0), tok(2), tok(3), tok(0),
            _resident(womla.shape), _resident(wolru.shape), _resident(wout.shape), _resident(g.shape),
        ],
        out_specs=tok(0),
        out_shape=jax.ShapeDtypeStruct((b, s, d), F32),
        compiler_params=pltpu.CompilerParams(
            dimension_semantics=("arbitrary", "arbitrary"), vmem_limit_bytes=VMEM_LIMIT),
        name="merge",
    )(ot, lg, z, z, h, womla, wolru, wout, g)


def _pad_heads(w, per_head):
    k = w.shape[0]
    w = w.reshape(k, MLA_HEADS, per_head)
    return jnp.pad(w, ((0, 0), (0, 0), (0, HEAD_PAD - per_head))).reshape(k, MLA_HEADS * HEAD_PAD)


def _rope_consts():
    half = QK_ROPE // 2
    inv_freq = ROPE_THETA ** (-jnp.arange(half, dtype=F32) / half)
    lane = jnp.arange(HEAD_PAD)
    lo = (lane >= QK_NOPE) & (lane < QK_NOPE + half)
    hi = (lane >= QK_NOPE + half) & (lane < QK_NOPE + QK_ROPE)
    freq = jnp.where(lo | hi, inv_freq[(lane - QK_NOPE) % half], 0.0)
    rows = jnp.stack([freq, jnp.where(lo, -1.0, 0.0), jnp.where(hi, 1.0, 0.0)]).astype(F32)
    return jnp.pad(rows, ((0, SUBLANES - rows.shape[0]), (0, 0)))


def kernel(x, positions, ffn1_pre_g, ffn1_w_gate, ffn1_w_up, ffn1_w_down, ffn1_post_g, mix_pre_g, w_in, q_norm_g, w_uq, kv_norm_g, w_ukv, w_o_mla, conv_w, conv_b, w_rg, b_rg, w_ig, b_ig, lru_lambda, w_o_lru, w_out, mix_post_g, ffn2_pre_g, ffn2_w_gate, ffn2_w_up, ffn2_w_down, ffn2_post_g):
    b, s, d = x.shape
    depth = w_in.shape[0]
    tm = min(512, s)
    tq = 2 * ATTN_TK
    tc = min(256, s)
    ff_chunk = ffn1_w_gate.shape[2] // 2
    q_scale = (QK_NOPE + QK_ROPE) ** -0.5 * math.log2(math.e)
    n_lat = Q_LORA + KV_LORA
    rope_consts = _rope_consts()
    vones = jnp.tile((jnp.arange(V_ROWS) == V_HEAD).astype(F32), MLA_HEADS).reshape(-1, 1)
    pos = positions.reshape(b, s, 1)
    row = lambda v: v.reshape(1, -1)

    h = x.reshape(b * s, d)
    for l in range(depth):
        h = _ffn(h, row(ffn1_pre_g[l]), ffn1_w_gate[l].astype(BF16), ffn1_w_up[l].astype(BF16),
                 ffn1_w_down[l].astype(BF16), row(ffn1_post_g[l]), tm=tm, ff_chunk=ff_chunk)

        wi = w_in[l]
        w_krope = jnp.pad(wi[:, n_lat:n_lat + QK_ROPE], ((0, 0), (QK_NOPE, HEAD_PAD - QK_NOPE - QK_ROPE)))
        wlat = jnp.concatenate([wi[:, :n_lat], w_krope], axis=1).astype(BF16)
        wrest = wi[:, n_lat + QK_ROPE:].astype(BF16)
        wuq = _pad_heads(w_uq[l], QK_NOPE + QK_ROPE).astype(BF16)
        wkv = w_ukv[l].reshape(KV_LORA, MLA_HEADS, QK_NOPE + V_HEAD)
        wk = _pad_heads(wkv[:, :, :QK_NOPE].reshape(KV_LORA, -1), QK_NOPE).astype(BF16)
        wvt = jnp.pad(wkv[:, :, QK_NOPE:], ((0, 0), (0, 0), (0, V_ROWS - V_HEAD)))
        wvt = wvt.reshape(KV_LORA, MLA_HEADS * V_ROWS).T.astype(BF16)
        h3 = h.reshape(b, s, d)
        q, k, vt, z = _inproj(h3, pos, row(mix_pre_g[l]), wlat, wrest, row(q_norm_g[l]), wuq,
                              row(kv_norm_g[l]), wk, wvt, vones, rope_consts, tm=tm, q_scale=q_scale)

        ot = _attn(q, k, vt, tq=tq)
        lg = _lru(z, conv_w[l], row(conv_b[l]), w_rg[l].astype(BF16), row(b_rg[l]),
                  w_ig[l].astype(BF16), row(b_ig[l]), row(lru_lambda[l]), tc=tc)

        h = _merge(ot, lg, z, h3, w_o_mla[l].astype(BF16), w_o_lru[l].astype(BF16), w_out[l].astype(BF16),
                   row(mix_post_g[l]), tm=tm).reshape(b * s, d)

        h = _ffn(h, row(ffn2_pre_g[l]), ffn2_w_gate[l].astype(BF16), ffn2_w_up[l].astype(BF16),
                 ffn2_w_down[l].astype(BF16), row(ffn2_post_g[l]), tm=tm, ff_chunk=ff_chunk)
    return h.reshape(b, s, d)
```

```python
import functools
import math

import jax
import jax.numpy as jnp
from jax import lax
from jax.experimental import pallas as pl
from jax.experimental.pallas import tpu as pltpu

MLA_HEADS = 8
QK_NOPE = 64
QK_ROPE = 32
V_HEAD = 64
Q_LORA = 384
KV_LORA = 256
ROPE_THETA = 10000.0
LRU_BLOCKS = 8
CONV_WIDTH = 4
LRU_C = 8.0
FFN_RES_WEIGHT = 0.5
NORM_EPS = 1e-6

LANES = 128
SUBLANES = 8
BF16_ROWS = 16
HEAD_PAD = LANES
V_ROWS = V_HEAD + BF16_ROWS
ATTN_TK = 256
ATTN_HEADS = 2
VMEM_LIMIT = 56 * 1024 * 1024
LOG2E = math.log2(math.e)

F32 = jnp.float32
BF16 = jnp.bfloat16
NT_DIMS = (((1,), (1,)), ((), ()))
TN_DIMS = (((0,), (0,)), ((), ()))


def _rms(x, g):
    return x * lax.rsqrt(jnp.mean(x * x, axis=-1, keepdims=True) + NORM_EPS) * g


def _dot(a, b):
    return jnp.dot(a, b, preferred_element_type=F32)


def _resident(shape):
    nd = len(shape)
    return pl.BlockSpec(shape, lambda *_: (0,) * nd, pipeline_mode=pl.Buffered(1))


def _ffn_kernel(h_ref, pre_g_ref, wg_ref, wu_ref, wd_ref, post_g_ref, o_ref, act_ref, *, ff_chunk):
    x = h_ref[...]
    xn = _rms(x, pre_g_ref[...]).astype(BF16)
    d_ff = wg_ref.shape[1]
    for c in range(d_ff // ff_chunk):
        sl = slice(c * ff_chunk, (c + 1) * ff_chunk)
        g = _dot(xn, wg_ref[:, sl])
        u = _dot(xn, wu_ref[:, sl])
        act_ref[:, sl] = (g * jax.nn.sigmoid(g) * u).astype(BF16)
    y = _dot(act_ref[...], wd_ref[...])
    o_ref[...] = x + FFN_RES_WEIGHT * _rms(y, post_g_ref[...])


def _ffn(h, pre_g, wg, wu, wd, post_g, *, tm, ff_chunk):
    t, d = h.shape
    d_ff = wg.shape[1]
    return pl.pallas_call(
        functools.partial(_ffn_kernel, ff_chunk=ff_chunk),
        grid=(t // tm,),
        in_specs=[
            pl.BlockSpec((tm, d), lambda i: (i, 0)),
            _resident((1, d)),
            _resident((d, d_ff)),
            _resident((d, d_ff)),
            _resident((d_ff, d)),
            _resident((1, d)),
        ],
        out_specs=pl.BlockSpec((tm, d), lambda i: (i, 0)),
        out_shape=jax.ShapeDtypeStruct((t, d), F32),
        scratch_shapes=[pltpu.VMEM((tm, d_ff), BF16)],
        compiler_params=pltpu.CompilerParams(
            dimension_semantics=("arbitrary",), vmem_limit_bytes=VMEM_LIMIT),
        name="ffn",
    )(h, pre_g, wg, wu, wd, post_g)


def _inproj_kernel(h_ref, pos_ref, g_ref, wlat_ref, wrest_ref, qg_ref, wuq_ref, kvg_ref, wk_ref,
                   wvt_ref, vones_ref, rope_ref, q_ref, k_ref, vt_ref, z_ref, *, q_scale):
    u = _rms(h_ref[0], g_ref[...]).astype(BF16)
    z_ref[0] = _dot(u, wrest_ref[...])
    lat = _dot(u, wlat_ref[...])
    qn = _rms(lat[:, :Q_LORA], qg_ref[...]).astype(BF16)
    kvn = _rms(lat[:, Q_LORA:Q_LORA + KV_LORA], kvg_ref[...]).astype(BF16)
    k_rope = lat[:, Q_LORA + KV_LORA:]
    q = _dot(qn, wuq_ref[...])
    k = _dot(kvn, wk_ref[...])
    vt = lax.dot_general(wvt_ref[...], kvn, NT_DIMS, preferred_element_type=F32) + vones_ref[...]

    ang = pos_ref[0].astype(F32) * rope_ref[0:1, :]
    cos = jnp.cos(ang)
    sin = jnp.sin(ang)
    sin_lo = sin * rope_ref[1:2, :]
    sin_hi = sin * rope_ref[2:3, :]
    half = QK_ROPE // 2

    def rope(r):
        return r * cos + pltpu.roll(r, HEAD_PAD - half, 1) * sin_lo + pltpu.roll(r, half, 1) * sin_hi

    k_rope = rope(k_rope)
    for h in range(q.shape[1] // HEAD_PAD):
        sl = slice(h * HEAD_PAD, (h + 1) * HEAD_PAD)
        q_ref[0, :, sl] = (rope(q[:, sl]) * q_scale).astype(BF16)
        k_ref[0, :, sl] = (k[:, sl] + k_rope).astype(BF16)
    for c in range(vt_ref.shape[1]):
        vt_ref[0, c] = vt[:, c * ATTN_TK:(c + 1) * ATTN_TK].astype(BF16)


def _inproj(h, pos, g, wlat, wrest, qg, wuq, kvg, wk, wvt, vones, rope_consts, *, tm, q_scale):
    b, s, d = h.shape
    hp = wuq.shape[1]
    n_rest = wrest.shape[1]
    vrows = wvt.shape[0]
    tok = lambda w: pl.BlockSpec((1, tm, w), lambda bi, i: (bi, i, 0))
    consts = (g, wlat, wrest, qg, wuq, kvg, wk, wvt, vones, rope_consts)
    return pl.pallas_call(
        functools.partial(_inproj_kernel, q_scale=q_scale),
        grid=(b, s // tm),
        in_specs=[tok(d), tok(1)] + [_resident(c.shape) for c in consts],
        out_specs=[tok(hp), tok(hp),
                   pl.BlockSpec((1, tm // ATTN_TK, vrows, ATTN_TK), lambda bi, i: (bi, i, 0, 0)),
                   tok(n_rest)],
        out_shape=[
            jax.ShapeDtypeStruct((b, s, hp), BF16),
            jax.ShapeDtypeStruct((b, s, hp), BF16),
            jax.ShapeDtypeStruct((b, s // ATTN_TK, vrows, ATTN_TK), BF16),
            jax.ShapeDtypeStruct((b, s, n_rest), F32),
        ],
        compiler_params=pltpu.CompilerParams(
            dimension_semantics=("arbitrary", "arbitrary"), vmem_limit_bytes=VMEM_LIMIT),
        name="inproj",
    )(h, pos, *consts)


def _attn_kernel(q_ref, k_ref, vt_ref, o_ref, *scratch, tq):
    tk = ATTN_TK
    nh = ATTN_HEADS
    qi = pl.program_id(2)
    per_tile = tq // tk
    s_even, s_odd, acc_refs = scratch[:nh], scratch[nh:2 * nh], scratch[2 * nh:]
    qs = [q_ref[0, :, h * HEAD_PAD:(h + 1) * HEAD_PAD] for h in range(nh)]

    def scores(h, blk):
        kb = k_ref[0, pl.ds(pl.multiple_of(blk * tk, tk), tk), h * HEAD_PAD:(h + 1) * HEAD_PAD]
        return lax.dot_general(kb, qs[h], NT_DIMS, preferred_element_type=F32)

    def prefetch(h, blk, s_ref):
        s = scores(h, blk)
        s_ref[...] = s
        return jnp.max(s, axis=0, keepdims=True)

    def consume(h, s, blk, bmax, m):
        m_new = jnp.maximum(m, bmax)
        alpha = jnp.exp2(m - m_new)
        p = jnp.exp2((s - m_new).astype(BF16))
        vt = vt_ref[0, blk, h * V_ROWS:(h + 1) * V_ROWS, :]
        acc_refs[h][...] = alpha * acc_refs[h][...] + _dot(vt, p)
        return m_new

    def pair(u, carry):
        ms, bmax_even = carry
        blk = per_tile * u
        bmax_odd = [prefetch(h, blk + 1, s_odd[h]) for h in range(nh)]
        ms = [consume(h, s_even[h][...], blk, bmax_even[h], ms[h]) for h in range(nh)]
        bmax_even = [prefetch(h, blk + 2, s_even[h]) for h in range(nh)]
        ms = [consume(h, s_odd[h][...], blk + 1, bmax_odd[h], ms[h]) for h in range(nh)]
        return tuple(ms), tuple(bmax_even)

    def two_pairs(u2, carry):
        return pair(2 * u2 + 1, pair(2 * u2, carry))

    for h in range(nh):
        acc_refs[h][...] = jnp.zeros((V_ROWS, tq), F32)
    carry = (tuple(jnp.full((1, tq), -jnp.inf, F32) for _ in range(nh)),
             tuple(prefetch(h, 0, s_even[h]) for h in range(nh)))
    carry = lax.fori_loop(0, qi >> 1, two_pairs, carry)
    ms, _ = lax.cond((qi & 1) == 1, lambda c: pair(qi - 1, c), lambda c: c, carry)

    row = lax.broadcasted_iota(jnp.int32, (tk, tq), 0)
    col = lax.broadcasted_iota(jnp.int32, (tk, tq), 1)
    first = per_tile * qi
    for h in range(nh):
        m = ms[h]
        for d in range(per_tile):
            s = s_even[h][...] if d == 0 else scores(h, first + d)
            s = jnp.where(row + d * tk <= col, s, -jnp.inf)
            m = consume(h, s, first + d, jnp.max(s, axis=0, keepdims=True), m)
        acc = acc_refs[h][...]
        o_ref[0, h * V_HEAD:(h + 1) * V_HEAD, :] = (acc[:V_HEAD] / acc[V_HEAD:V_HEAD + 1]).astype(BF16)


def _attn(q, k, vt, *, tq):
    b, s, hp = q.shape
    nh = ATTN_HEADS
    groups = hp // (nh * HEAD_PAD)
    assert tq == 2 * ATTN_TK, "the key loop consumes blocks in pairs"
    return pl.pallas_call(
        functools.partial(_attn_kernel, tq=tq),
        grid=(b, groups, s // tq),
        in_specs=[
            pl.BlockSpec((1, tq, nh * HEAD_PAD), lambda bi, g, i: (bi, i, g)),
            pl.BlockSpec((1, s, nh * HEAD_PAD), lambda bi, g, i: (bi, 0, g)),
            pl.BlockSpec((1, s // ATTN_TK, nh * V_ROWS, ATTN_TK), lambda bi, g, i: (bi, 0, g, 0)),
        ],
        out_specs=pl.BlockSpec((1, nh * V_HEAD, tq), lambda bi, g, i: (bi, g, i)),
        out_shape=jax.ShapeDtypeStruct((b, groups * nh * V_HEAD, s), BF16),
        scratch_shapes=([pltpu.VMEM((ATTN_TK, tq), F32)] * (2 * nh) + [pltpu.VMEM((V_ROWS, tq), F32)] * nh),
        compiler_params=pltpu.CompilerParams(
            dimension_semantics=("arbitrary", "arbitrary", "arbitrary"), vmem_limit_bytes=VMEM_LIMIT),
        name="attn",
    )(q, k, vt)


def _lru_kernel(xb_ref, yb_ref, cw_ref, cb_ref, wrg_ref, brg_ref, wig_ref, big_ref, lam_ref, o_ref,
                xbuf, a_ref, b_ref, hcar, *, tc):
    t = pl.program_id(1)
    halo = SUBLANES

    @pl.when(t == 0)
    def _():
        xbuf[0:halo, :] = jnp.zeros((halo, xbuf.shape[1]), F32)
        hcar[...] = jnp.zeros(hcar.shape, F32)

    xbuf[halo:halo + tc, :] = xb_ref[0]
    xc = cb_ref[...]
    for kk in range(CONV_WIDTH):
        xc = xc + xbuf[pl.ds(halo - (CONV_WIDTH - 1) + kk, tc), :] * cw_ref[kk:kk + 1, :]
    xbuf[0:halo, :] = xbuf[tc:tc + halo, :]

    half_rate = (-0.5 * LRU_C * LOG2E) * jax.nn.softplus(-lam_ref[...])
    width = xc.shape[1]
    blk = width // LRU_BLOCKS
    for g in range(LRU_BLOCKS):
        sl = slice(g * blk, (g + 1) * blk)
        xg = xc[:, sl]
        xg16 = xg.astype(BF16)
        t_r = jnp.tanh(_dot(xg16, wrg_ref[g]) + brg_ref[:, sl])
        t_i = jnp.tanh(_dot(xg16, wig_ref[g]) + big_ref[:, sl])
        a = jnp.exp2(half_rate[:, sl] * t_r + half_rate[:, sl])
        a_ref[:, sl] = a
        b_ref[:, sl] = jnp.sqrt(1.0 - a * a) * ((0.5 * t_i + 0.5) * xg)

    row = lax.broadcasted_iota(jnp.int32, (SUBLANES, width), 0)

    def group(i, h_prev):
        r0 = pl.multiple_of(i * SUBLANES, SUBLANES)
        a = a_ref[pl.ds(r0, SUBLANES), :]
        bb = b_ref[pl.ds(r0, SUBLANES), :]
        for sh in (1, 2, 4):
            a_sh = jnp.where(row >= sh, pltpu.roll(a, sh, 0), 1.0)
            b_sh = jnp.where(row >= sh, pltpu.roll(bb, sh, 0), 0.0)
            bb = a * b_sh + bb
            a = a * a_sh
        hh = a * h_prev + bb
        b_ref[pl.ds(r0, SUBLANES), :] = hh
        return jnp.broadcast_to(hh[SUBLANES - 1:SUBLANES, :], hh.shape)

    hcar[...] = lax.fori_loop(0, tc // SUBLANES, group, hcar[...])
    o_ref[0] = (b_ref[...] * jax.nn.gelu(yb_ref[0])).astype(BF16)


def _lru(z, cw, cb, wrg, brg, wig, big, lam, *, tc):
    b, s, _ = z.shape
    w = cw.shape[1]
    return pl.pallas_call(
        functools.partial(_lru_kernel, tc=tc),
        grid=(b, s // tc),
        in_specs=[
            pl.BlockSpec((1, tc, w), lambda bi, i: (bi, i, 0)),
            pl.BlockSpec((1, tc, w), lambda bi, i: (bi, i, 1)),
            _resident(cw.shape), _resident(cb.shape), _resident(wrg.shape), _resident(brg.shape),
            _resident(wig.shape), _resident(big.shape), _resident(lam.shape),
        ],
        out_specs=pl.BlockSpec((1, tc, w), lambda bi, i: (bi, i, 0)),
        out_shape=jax.ShapeDtypeStruct((b, s, w), BF16),
        scratch_shapes=[
            pltpu.VMEM((tc + SUBLANES, w), F32),
            pltpu.VMEM((tc, w), F32),
            pltpu.VMEM((tc, w), F32),
            pltpu.VMEM((SUBLANES, w), F32),
        ],
        compiler_params=pltpu.CompilerParams(
            dimension_semantics=("arbitrary", "arbitrary"), vmem_limit_bytes=VMEM_LIMIT),
        name="lru",
    )(z, z, cw, cb, wrg, brg, wig, big, lam)


def _merge_kernel(ot_ref, lg_ref, ga_ref, gb_ref, h_ref, womla_ref, wolru_ref, wout_ref, g_ref, out_ref):
    o_mla = lax.dot_general(ot_ref[0], womla_ref[...], TN_DIMS, preferred_element_type=F32)
    o_lru = _dot(lg_ref[0], wolru_ref[...])
    merged = jax.nn.sigmoid(ga_ref[0]) * o_mla + jax.nn.sigmoid(gb_ref[0]) * o_lru
    y = _dot(merged.astype(BF16), wout_ref[...])
    out_ref[0] = h_ref[0] + _rms(y, g_ref[...])


def _merge(ot, lg, z, h, womla, wolru, wout, g, *, tm):
    b, s, d = h.shape
    tok = lambda col: pl.BlockSpec((1, tm, d), lambda bi, i: (bi, i, col))
    return pl.pallas_call(
        _merge_kernel,
        grid=(b, s // tm),
        in_specs=[
            pl.BlockSpec((1, ot.shape[1], tm), lambda bi, i: (bi, 0, i)),
            tok(0), tok(2), tok(3), tok(0),
            _resident(womla.shape), _resident(wolru.shape), _resident(wout.shape), _resident(g.shape),
        ],
        out_specs=tok(0),
        out_shape=jax.ShapeDtypeStruct((b, s, d), F32),
        compiler_params=pltpu.CompilerParams(
            dimension_semantics=("arbitrary", "arbitrary"), vmem_limit_bytes=VMEM_LIMIT),
        name="merge",
    )(ot, lg, z, z, h, womla, wolru, wout, g)


def _pad_heads(w, per_head):
    k = w.shape[0]
    w = w.reshape(k, MLA_HEADS, per_head)
    return jnp.pad(w, ((0, 0), (0, 0), (0, HEAD_PAD - per_head))).reshape(k, MLA_HEADS * HEAD_PAD)


def _rope_consts():
    half = QK_ROPE // 2
    inv_freq = ROPE_THETA ** (-jnp.arange(half, dtype=F32) / half)
    lane = jnp.arange(HEAD_PAD)
    lo = (lane >= QK_NOPE) & (lane < QK_NOPE + half)
    hi = (lane >= QK_NOPE + half) & (lane < QK_NOPE + QK_ROPE)
    freq = jnp.where(lo | hi, inv_freq[(lane - QK_NOPE) % half], 0.0)
    rows = jnp.stack([freq, jnp.where(lo, -1.0, 0.0), jnp.where(hi, 1.0, 0.0)]).astype(F32)
    return jnp.pad(rows, ((0, SUBLANES - rows.shape[0]), (0, 0)))


def kernel(x, positions, ffn1_pre_g, ffn1_w_gate, ffn1_w_up, ffn1_w_down, ffn1_post_g, mix_pre_g, w_in, q_norm_g, w_uq, kv_norm_g, w_ukv, w_o_mla, conv_w, conv_b, w_rg, b_rg, w_ig, b_ig, lru_lambda, w_o_lru, w_out, mix_post_g, ffn2_pre_g, ffn2_w_gate, ffn2_w_up, ffn2_w_down, ffn2_post_g):
    b, s, d = x.shape
    depth = w_in.shape[0]
    tm = min(512, s)
    tq = 2 * ATTN_TK
    tc = min(512, s)
    ff_chunk = ffn1_w_gate.shape[2] // 2
    q_scale = (QK_NOPE + QK_ROPE) ** -0.5 * LOG2E
    n_lat = Q_LORA + KV_LORA
    rope_consts = _rope_consts()
    vones = jnp.tile((jnp.arange(V_ROWS) == V_HEAD).astype(F32), MLA_HEADS).reshape(-1, 1)
    pos = positions.reshape(b, s, 1)
    row = lambda v: v.reshape(1, -1)

    h = x.reshape(b * s, d)
    for l in range(depth):
        h = _ffn(h, row(ffn1_pre_g[l]), ffn1_w_gate[l].astype(BF16), ffn1_w_up[l].astype(BF16),
                 ffn1_w_down[l].astype(BF16), row(ffn1_post_g[l]), tm=tm, ff_chunk=ff_chunk)

        wi = w_in[l]
        w_krope = jnp.pad(wi[:, n_lat:n_lat + QK_ROPE], ((0, 0), (QK_NOPE, HEAD_PAD - QK_NOPE - QK_ROPE)))
        wlat = jnp.concatenate([wi[:, :n_lat], w_krope], axis=1).astype(BF16)
        wrest = wi[:, n_lat + QK_ROPE:].astype(BF16)
        wuq = _pad_heads(w_uq[l], QK_NOPE + QK_ROPE).astype(BF16)
        wkv = w_ukv[l].reshape(KV_LORA, MLA_HEADS, QK_NOPE + V_HEAD)
        wk = _pad_heads(wkv[:, :, :QK_NOPE].reshape(KV_LORA, -1), QK_NOPE).astype(BF16)
        wvt = jnp.pad(wkv[:, :, QK_NOPE:], ((0, 0), (0, 0), (0, V_ROWS - V_HEAD)))
        wvt = wvt.reshape(KV_LORA, MLA_HEADS * V_ROWS).T.astype(BF16)
        h3 = h.reshape(b, s, d)
        q, k, vt, z = _inproj(h3, pos, row(mix_pre_g[l]), wlat, wrest, row(q_norm_g[l]), wuq,
                              row(kv_norm_g[l]), wk, wvt, vones, rope_consts, tm=tm, q_scale=q_scale)

        ot = _attn(q, k, vt, tq=tq)
        lg = _lru(z, conv_w[l], row(conv_b[l]), (0.5 * w_rg[l]).astype(BF16), row(0.5 * b_rg[l]),
                  (0.5 * w_ig[l]).astype(BF16), row(0.5 * b_ig[l]), row(lru_lambda[l]), tc=tc)

        h = _merge(ot, lg, z, h3, w_o_mla[l].astype(BF16), w_o_lru[l].astype(BF16), w_out[l].astype(BF16),
                   row(mix_post_g[l]), tm=tm).reshape(b * s, d)

        h = _ffn(h, row(ffn2_pre_g[l]), ffn2_w_gate[l].astype(BF16), ffn2_w_up[l].astype(BF16),
                 ffn2_w_down[l].astype(BF16), row(ffn2_post_g[l]), tm=tm, ff_chunk=ff_chunk)
    return h.reshape(b, s, d)
```

```python
import functools
import math

import jax
import jax.numpy as jnp
from jax import lax
from jax.experimental import pallas as pl
from jax.experimental.pallas import tpu as pltpu

MLA_HEADS = 8
QK_NOPE = 64
QK_ROPE = 32
V_HEAD = 64
Q_LORA = 384
KV_LORA = 256
ROPE_THETA = 10000.0
LRU_BLOCKS = 8
CONV_WIDTH = 4
LRU_C = 8.0
FFN_RES_WEIGHT = 0.5
NORM_EPS = 1e-6

LANES = 128
SUBLANES = 8
BF16_ROWS = 16
HEAD_PAD = LANES
V_ROWS = V_HEAD + BF16_ROWS
ATTN_TK = 256
ATTN_HEADS = 2
VMEM_LIMIT = 56 * 1024 * 1024
LOG2E = math.log2(math.e)

F32 = jnp.float32
BF16 = jnp.bfloat16
NT_DIMS = (((1,), (1,)), ((), ()))


def _rms(x, g):
    return x * lax.rsqrt(jnp.mean(x * x, axis=-1, keepdims=True) + NORM_EPS) * g


def _dot(a, b):
    return jnp.dot(a, b, preferred_element_type=F32)


def _resident(shape):
    nd = len(shape)
    return pl.BlockSpec(shape, lambda *_: (0,) * nd, pipeline_mode=pl.Buffered(1))


def _ffn_kernel(h_ref, pre_g_ref, wg_ref, wu_ref, wd_ref, post_g_ref, o_ref, act_ref, *, ff_chunk):
    x = h_ref[...]
    xn = _rms(x, pre_g_ref[...]).astype(BF16)
    d_ff = wg_ref.shape[1]
    for c in range(d_ff // ff_chunk):
        sl = slice(c * ff_chunk, (c + 1) * ff_chunk)
        g = _dot(xn, wg_ref[:, sl])
        u = _dot(xn, wu_ref[:, sl])
        act_ref[:, sl] = (g * jax.nn.sigmoid(g) * u).astype(BF16)
    y = _dot(act_ref[...], wd_ref[...])
    o_ref[...] = x + FFN_RES_WEIGHT * _rms(y, post_g_ref[...])


def _ffn(h, pre_g, wg, wu, wd, post_g, *, tm, ff_chunk):
    t, d = h.shape
    d_ff = wg.shape[1]
    return pl.pallas_call(
        functools.partial(_ffn_kernel, ff_chunk=ff_chunk),
        grid=(t // tm,),
        in_specs=[
            pl.BlockSpec((tm, d), lambda i: (i, 0)),
            _resident((1, d)),
            _resident((d, d_ff)),
            _resident((d, d_ff)),
            _resident((d_ff, d)),
            _resident((1, d)),
        ],
        out_specs=pl.BlockSpec((tm, d), lambda i: (i, 0)),
        out_shape=jax.ShapeDtypeStruct((t, d), F32),
        scratch_shapes=[pltpu.VMEM((tm, d_ff), BF16)],
        compiler_params=pltpu.CompilerParams(
            dimension_semantics=("arbitrary",), vmem_limit_bytes=VMEM_LIMIT),
        name="ffn",
    )(h, pre_g, wg, wu, wd, post_g)


def _inproj_kernel(h_ref, pos_col_ref, pos_row_ref, g_ref, wlat_ref, wrest_ref, qg_ref, wuqt_ref, kvg_ref,
                   wk_ref, wvt_ref, vones_ref, rope_ref, freq_col_ref, qt_ref, k_ref, vt_ref, z_ref, *,
                   q_scale):
    u = _rms(h_ref[0], g_ref[...]).astype(BF16)
    z_ref[0] = _dot(u, wrest_ref[...])
    lat = _dot(u, wlat_ref[...])
    qn = _rms(lat[:, :Q_LORA], qg_ref[...]).astype(BF16)
    kvn = _rms(lat[:, Q_LORA:Q_LORA + KV_LORA], kvg_ref[...]).astype(BF16)
    k_rope = lat[:, Q_LORA + KV_LORA:]
    k = _dot(kvn, wk_ref[...])
    qt = lax.dot_general(wuqt_ref[...], qn, NT_DIMS, preferred_element_type=F32)
    vt = lax.dot_general(wvt_ref[...], kvn, NT_DIMS, preferred_element_type=F32) + vones_ref[...]
    half = QK_ROPE // 2

    ang_t = freq_col_ref[...] * pos_row_ref[0].astype(F32)
    cos_t = jnp.cos(ang_t)
    sin_t = jnp.sin(ang_t)
    for h in range(qt.shape[0] // HEAD_PAD):
        base = h * HEAD_PAD
        x1 = qt[base + QK_NOPE:base + QK_NOPE + half]
        x2 = qt[base + QK_NOPE + half:base + QK_NOPE + QK_ROPE]
        tile = jnp.concatenate([qt[base:base + QK_NOPE], x1 * cos_t - x2 * sin_t, x2 * cos_t + x1 * sin_t,
                                qt[base + QK_NOPE + QK_ROPE:base + HEAD_PAD]], axis=0)
        qt_ref[0, base:base + HEAD_PAD, :] = (tile * q_scale).astype(BF16)

    ang = pos_col_ref[0].astype(F32) * rope_ref[0:1, :]
    cos = jnp.cos(ang)
    sin = jnp.sin(ang)
    sin_lo = sin * rope_ref[1:2, :]
    sin_hi = sin * rope_ref[2:3, :]
    k_rope = (k_rope * cos + pltpu.roll(k_rope, HEAD_PAD - half, 1) * sin_lo
              + pltpu.roll(k_rope, half, 1) * sin_hi)
    for h in range(k.shape[1] // HEAD_PAD):
        sl = slice(h * HEAD_PAD, (h + 1) * HEAD_PAD)
        k_ref[0, :, sl] = (k[:, sl] + k_rope).astype(BF16)
    for c in range(vt_ref.shape[1]):
        vt_ref[0, c] = vt[:, c * ATTN_TK:(c + 1) * ATTN_TK].astype(BF16)


def _inproj(h, pos_col, pos_row, g, wlat, wrest, qg, wuqt, kvg, wk, wvt, vones, rope_consts, freq_col, *,
            tm, q_scale):
    b, s, d = h.shape
    hp = wuqt.shape[0]
    n_rest = wrest.shape[1]
    vrows = wvt.shape[0]
    tok = lambda w: pl.BlockSpec((1, tm, w), lambda bi, i: (bi, i, 0))
    consts = (g, wlat, wrest, qg, wuqt, kvg, wk, wvt, vones, rope_consts, freq_col)
    return pl.pallas_call(
        functools.partial(_inproj_kernel, q_scale=q_scale),
        grid=(b, s // tm),
        in_specs=[tok(d), tok(1), pl.BlockSpec((1, 1, tm), lambda bi, i: (bi, 0, i))]
                 + [_resident(c.shape) for c in consts],
        out_specs=[pl.BlockSpec((1, hp, tm), lambda bi, i: (bi, 0, i)), tok(hp),
                   pl.BlockSpec((1, tm // ATTN_TK, vrows, ATTN_TK), lambda bi, i: (bi, i, 0, 0)),
                   tok(n_rest)],
        out_shape=[
            jax.ShapeDtypeStruct((b, hp, s), BF16),
            jax.ShapeDtypeStruct((b, s, hp), BF16),
            jax.ShapeDtypeStruct((b, s // ATTN_TK, vrows, ATTN_TK), BF16),
            jax.ShapeDtypeStruct((b, s, n_rest), F32),
        ],
        compiler_params=pltpu.CompilerParams(
            dimension_semantics=("arbitrary", "arbitrary"), vmem_limit_bytes=VMEM_LIMIT),
        name="inproj",
    )(h, pos_col, pos_row, *consts)


def _attn_kernel(qt_ref, k_ref, vt_ref, o_ref, *scratch, tq):
    tk = ATTN_TK
    nh = ATTN_HEADS
    qi = pl.program_id(2)
    per_tile = tq // tk
    s_even, s_odd, acc_refs = scratch[:nh], scratch[nh:2 * nh], scratch[2 * nh:]
    qts = [qt_ref[0, h * HEAD_PAD:(h + 1) * HEAD_PAD, :] for h in range(nh)]

    def prefetch(h, blk, s_ref):
        kb = k_ref[0, pl.ds(pl.multiple_of(blk * tk, tk), tk), h * HEAD_PAD:(h + 1) * HEAD_PAD]
        s = _dot(kb, qts[h])
        s_ref[...] = s
        return jnp.max(s, axis=0, keepdims=True)

    def values_t(h, blk):
        return vt_ref[0, blk, h * V_ROWS:(h + 1) * V_ROWS, :]

    def consume(h, s, blk, bmax, m):
        m_new = jnp.maximum(m, bmax)
        alpha = jnp.exp2(m - m_new)
        p = jnp.exp2((s - m_new).astype(BF16))
        acc_refs[h][...] = alpha * acc_refs[h][...] + _dot(values_t(h, blk), p)
        return m_new

    def pair(u, carry):
        ms, bmax_even = carry
        blk = per_tile * u
        bmax_odd = [prefetch(h, blk + 1, s_odd[h]) for h in range(nh)]
        ms = [consume(h, s_even[h][...], blk, bmax_even[h], ms[h]) for h in range(nh)]
        bmax_even = [prefetch(h, blk + 2, s_even[h]) for h in range(nh)]
        ms = [consume(h, s_odd[h][...], blk + 1, bmax_odd[h], ms[h]) for h in range(nh)]
        return tuple(ms), tuple(bmax_even)

    def two_pairs(u2, carry):
        return pair(2 * u2 + 1, pair(2 * u2, carry))

    for h in range(nh):
        acc_refs[h][...] = jnp.zeros((V_ROWS, tq), F32)
    carry = (tuple(jnp.full((1, tq), -jnp.inf, F32) for _ in range(nh)),
             tuple(prefetch(h, 0, s_even[h]) for h in range(nh)))
    carry = lax.fori_loop(0, qi >> 1, two_pairs, carry)
    ms, _ = lax.cond((qi & 1) == 1, lambda c: pair(qi - 1, c), lambda c: c, carry)

    half = tq // 2
    causal = (lax.broadcasted_iota(jnp.int32, (tk, tq), 0) <= lax.broadcasted_iota(jnp.int32, (tk, tq), 1))
    first = per_tile * qi
    outs = []
    for h in range(nh):
        kb = k_ref[0, pl.ds(pl.multiple_of((first + 1) * tk, tk), tk), h * HEAD_PAD:(h + 1) * HEAD_PAD]
        s0 = jnp.where(causal, s_even[h][...], -jnp.inf)
        s1 = jnp.where(causal[:, :half], _dot(kb, qts[h][:, half:]), -jnp.inf)
        m_new = jnp.maximum(ms[h], jnp.max(s0, axis=0, keepdims=True))
        m_hi = jnp.maximum(m_new[:, half:], jnp.max(s1, axis=0, keepdims=True))
        m_new = jnp.concatenate([m_new[:, :half], m_hi], axis=1)
        p0 = jnp.exp2((s0 - m_new).astype(BF16))
        p1 = jnp.exp2((s1 - m_hi).astype(BF16))
        acc = jnp.exp2(ms[h] - m_new) * acc_refs[h][...] + _dot(values_t(h, first), p0)
        acc = jnp.concatenate([acc[:, :half], acc[:, half:] + _dot(values_t(h, first + 1), p1)], axis=1)
        outs.append(acc[:V_HEAD] / acc[V_HEAD:V_HEAD + 1])
    o_ref[0] = jnp.concatenate(outs, axis=0).T.astype(BF16)


def _attn(qt, k, vt, *, tq):
    b, s, hp = k.shape
    nh = ATTN_HEADS
    groups = hp // (nh * HEAD_PAD)
    assert tq == 2 * ATTN_TK, "the key loop consumes blocks in pairs"
    return pl.pallas_call(
        functools.partial(_attn_kernel, tq=tq),
        grid=(b, groups, s // tq),
        in_specs=[
            pl.BlockSpec((1, nh * HEAD_PAD, tq), lambda bi, g, i: (bi, g, i)),
            pl.BlockSpec((1, s, nh * HEAD_PAD), lambda bi, g, i: (bi, 0, g)),
            pl.BlockSpec((1, s // ATTN_TK, nh * V_ROWS, ATTN_TK), lambda bi, g, i: (bi, 0, g, 0)),
        ],
        out_specs=pl.BlockSpec((1, tq, nh * V_HEAD), lambda bi, g, i: (bi, i, g)),
        out_shape=jax.ShapeDtypeStruct((b, s, groups * nh * V_HEAD), BF16),
        scratch_shapes=([pltpu.VMEM((ATTN_TK, tq), F32)] * (2 * nh) + [pltpu.VMEM((V_ROWS, tq), F32)] * nh),
        compiler_params=pltpu.CompilerParams(
            dimension_semantics=("arbitrary", "arbitrary", "arbitrary"), vmem_limit_bytes=VMEM_LIMIT),
        name="attn",
    )(qt, k, vt)


def _lru_kernel(xb_ref, yb_ref, cw_ref, cb_ref, wrg_ref, brg_ref, wig_ref, big_ref, lam_ref, o_ref,
                xbuf, a_ref, b_ref, hcar, *, tc):
    t = pl.program_id(1)
    halo = SUBLANES

    @pl.when(t == 0)
    def _():
        xbuf[0:halo, :] = jnp.zeros((halo, xbuf.shape[1]), F32)
        hcar[...] = jnp.zeros(hcar.shape, F32)

    xbuf[halo:halo + tc, :] = xb_ref[0]
    xc = cb_ref[...]
    for kk in range(CONV_WIDTH):
        xc = xc + xbuf[pl.ds(halo - (CONV_WIDTH - 1) + kk, tc), :] * cw_ref[kk:kk + 1, :]
    xbuf[0:halo, :] = xbuf[tc:tc + halo, :]

    half_rate = (-0.5 * LRU_C * LOG2E) * jax.nn.softplus(-lam_ref[...])
    width = xc.shape[1]
    blk = width // LRU_BLOCKS
    for g in range(LRU_BLOCKS):
        sl = slice(g * blk, (g + 1) * blk)
        xg = xc[:, sl]
        xg16 = xg.astype(BF16)
        t_r = jnp.tanh(_dot(xg16, wrg_ref[g]) + brg_ref[:, sl])
        t_i = jnp.tanh(_dot(xg16, wig_ref[g]) + big_ref[:, sl])
        a = jnp.exp2(half_rate[:, sl] * t_r + half_rate[:, sl])
        a_ref[:, sl] = a
        b_ref[:, sl] = jnp.sqrt(1.0 - a * a) * ((0.5 * t_i + 0.5) * xg)

    row = lax.broadcasted_iota(jnp.int32, (SUBLANES, width), 0)

    def group(i, h_prev):
        r0 = pl.multiple_of(i * SUBLANES, SUBLANES)
        a = a_ref[pl.ds(r0, SUBLANES), :]
        bb = b_ref[pl.ds(r0, SUBLANES), :]
        for sh in (1, 2, 4):
            a_sh = jnp.where(row >= sh, pltpu.roll(a, sh, 0), 1.0)
            b_sh = jnp.where(row >= sh, pltpu.roll(bb, sh, 0), 0.0)
            bb = a * b_sh + bb
            a = a * a_sh
        hh = a * h_prev + bb
        b_ref[pl.ds(r0, SUBLANES), :] = hh
        return jnp.broadcast_to(hh[SUBLANES - 1:SUBLANES, :], hh.shape)

    hcar[...] = lax.fori_loop(0, tc // SUBLANES, group, hcar[...])
    o_ref[0] = (b_ref[...] * jax.nn.gelu(yb_ref[0])).astype(BF16)


def _lru(z, cw, cb, wrg, brg, wig, big, lam, *, tc):
    b, s, _ = z.shape
    w = cw.shape[1]
    return pl.pallas_call(
        functools.partial(_lru_kernel, tc=tc),
        grid=(b, s // tc),
        in_specs=[
            pl.BlockSpec((1, tc, w), lambda bi, i: (bi, i, 0)),
            pl.BlockSpec((1, tc, w), lambda bi, i: (bi, i, 1)),
            _resident(cw.shape), _resident(cb.shape), _resident(wrg.shape), _resident(brg.shape),
            _resident(wig.shape), _resident(big.shape), _resident(lam.shape),
        ],
        out_specs=pl.BlockSpec((1, tc, w), lambda bi, i: (bi, i, 0)),
        out_shape=jax.ShapeDtypeStruct((b, s, w), BF16),
        scratch_shapes=[
            pltpu.VMEM((tc + SUBLANES, w), F32),
            pltpu.VMEM((tc, w), F32),
            pltpu.VMEM((tc, w), F32),
            pltpu.VMEM((SUBLANES, w), F32),
        ],
        compiler_params=pltpu.CompilerParams(
            dimension_semantics=("arbitrary", "arbitrary"), vmem_limit_bytes=VMEM_LIMIT),
        name="lru",
    )(z, z, cw, cb, wrg, brg, wig, big, lam)


def _merge_kernel(o_ref, lg_ref, ga_ref, gb_ref, h_ref, womla_ref, wolru_ref, wout_ref, g_ref, out_ref):
    o_mla = _dot(o_ref[0], womla_ref[...])
    o_lru = _dot(lg_ref[0], wolru_ref[...])
    merged = jax.nn.sigmoid(ga_ref[0]) * o_mla + jax.nn.sigmoid(gb_ref[0]) * o_lru
    y = _dot(merged.astype(BF16), wout_ref[...])
    out_ref[0] = h_ref[0] + _rms(y, g_ref[...])


def _merge(o, lg, z, h, womla, wolru, wout, g, *, tm):
    b, s, d = h.shape
    tok = lambda col: pl.BlockSpec((1, tm, d), lambda bi, i: (bi, i, col))
    return pl.pallas_call(
        _merge_kernel,
        grid=(b, s // tm),
        in_specs=[
            pl.BlockSpec((1, tm, o.shape[2]), lambda bi, i: (bi, i, 0)),
            tok(0), tok(2), tok(3), tok(0),
            _resident(womla.shape), _resident(wolru.shape), _resident(wout.shape), _resident(g.shape),
        ],
        out_specs=tok(0),
        out_shape=jax.ShapeDtypeStruct((b, s, d), F32),
        compiler_params=pltpu.CompilerParams(
            dimension_semantics=("arbitrary", "arbitrary"), vmem_limit_bytes=VMEM_LIMIT),
        name="merge",
    )(o, lg, z, z, h, womla, wolru, wout, g)


def _pad_heads(w, per_head):
    k = w.shape[0]
    w = w.reshape(k, MLA_HEADS, per_head)
    return jnp.pad(w, ((0, 0), (0, 0), (0, HEAD_PAD - per_head))).reshape(k, MLA_HEADS * HEAD_PAD)


def _rope_consts():
    half = QK_ROPE // 2
    inv_freq = ROPE_THETA ** (-jnp.arange(half, dtype=F32) / half)
    lane = jnp.arange(HEAD_PAD)
    lo = (lane >= QK_NOPE) & (lane < QK_NOPE + half)
    hi = (lane >= QK_NOPE + half) & (lane < QK_NOPE + QK_ROPE)
    freq = jnp.where(lo | hi, inv_freq[(lane - QK_NOPE) % half], 0.0)
    rows = jnp.stack([freq, jnp.where(lo, -1.0, 0.0), jnp.where(hi, 1.0, 0.0)]).astype(F32)
    return jnp.pad(rows, ((0, SUBLANES - rows.shape[0]), (0, 0)))


def kernel(x, positions, ffn1_pre_g, ffn1_w_gate, ffn1_w_up, ffn1_w_down, ffn1_post_g, mix_pre_g, w_in, q_norm_g, w_uq, kv_norm_g, w_ukv, w_o_mla, conv_w, conv_b, w_rg, b_rg, w_ig, b_ig, lru_lambda, w_o_lru, w_out, mix_post_g, ffn2_pre_g, ffn2_w_gate, ffn2_w_up, ffn2_w_down, ffn2_post_g):
    b, s, d = x.shape
    depth = w_in.shape[0]
    tm = min(512, s)
    tq = 2 * ATTN_TK
    tc = min(512, s)
    ff_chunk = ffn1_w_gate.shape[2] // 2
    q_scale = (QK_NOPE + QK_ROPE) ** -0.5 * LOG2E
    n_lat = Q_LORA + KV_LORA
    rope_consts = _rope_consts()
    vones = jnp.tile((jnp.arange(V_ROWS) == V_HEAD).astype(F32), MLA_HEADS).reshape(-1, 1)
    pos_col = positions.reshape(b, s, 1)
    pos_row = positions.reshape(b, 1, s)
    half = QK_ROPE // 2
    freq_col = (ROPE_THETA ** (-jnp.arange(half, dtype=F32) / half)).reshape(half, 1)
    row = lambda v: v.reshape(1, -1)

    h = x.reshape(b * s, d)
    for l in range(depth):
        h = _ffn(h, row(ffn1_pre_g[l]), ffn1_w_gate[l].astype(BF16), ffn1_w_up[l].astype(BF16),
                 ffn1_w_down[l].astype(BF16), row(ffn1_post_g[l]), tm=tm, ff_chunk=ff_chunk)

        wi = w_in[l]
        w_krope = jnp.pad(wi[:, n_lat:n_lat + QK_ROPE], ((0, 0), (QK_NOPE, HEAD_PAD - QK_NOPE - QK_ROPE)))
        wlat = jnp.concatenate([wi[:, :n_lat], w_krope], axis=1).astype(BF16)
        wrest = wi[:, n_lat + QK_ROPE:].astype(BF16)
        wuqt = _pad_heads(w_uq[l], QK_NOPE + QK_ROPE).T.astype(BF16)
        wkv = w_ukv[l].reshape(KV_LORA, MLA_HEADS, QK_NOPE + V_HEAD)
        wk = _pad_heads(wkv[:, :, :QK_NOPE].reshape(KV_LORA, -1), QK_NOPE).astype(BF16)
        wvt = jnp.pad(wkv[:, :, QK_NOPE:], ((0, 0), (0, 0), (0, V_ROWS - V_HEAD)))
        wvt = wvt.reshape(KV_LORA, MLA_HEADS * V_ROWS).T.astype(BF16)
        h3 = h.reshape(b, s, d)
        qt, k, vt, z = _inproj(h3, pos_col, pos_row, row(mix_pre_g[l]), wlat, wrest, row(q_norm_g[l]), wuqt,
                               row(kv_norm_g[l]), wk, wvt, vones, rope_consts, freq_col,
                               tm=tm, q_scale=q_scale)

        o = _attn(qt, k, vt, tq=tq)
        lg = _lru(z, conv_w[l], row(conv_b[l]), (0.5 * w_rg[l]).astype(BF16), row(0.5 * b_rg[l]),
                  (0.5 * w_ig[l]).astype(BF16), row(0.5 * b_ig[l]), row(lru_lambda[l]), tc=tc)

        h = _merge(o, lg, z, h3, w_o_mla[l].astype(BF16), w_o_lru[l].astype(BF16), w_out[l].astype(BF16),
                   row(mix_post_g[l]), tm=tm).reshape(b * s, d)

        h = _ffn(h, row(ffn2_pre_g[l]), ffn2_w_gate[l].astype(BF16), ffn2_w_up[l].astype(BF16),
                 ffn2_w_down[l].astype(BF16), row(ffn2_post_g[l]), tm=tm, ff_chunk=ff_chunk)
    return h.reshape(b, s, d)
```

```python
import functools
import math

import jax
import jax.numpy as jnp
from jax import lax
from jax.experimental import pallas as pl
from jax.experimental.pallas import tpu as pltpu

MLA_HEADS = 8
QK_NOPE = 64
QK_ROPE = 32
V_HEAD = 64
Q_LORA = 384
KV_LORA = 256
ROPE_THETA = 10000.0
LRU_BLOCKS = 8
CONV_WIDTH = 4
LRU_C = 8.0
FFN_RES_WEIGHT = 0.5
NORM_EPS = 1e-6

LANES = 128
SUBLANES = 8
BF16_ROWS = 16
HEAD_PAD = LANES
V_ROWS = V_HEAD + BF16_ROWS
ATTN_TK = 256
ATTN_HEADS = 2
LRU_UNROLL = 16
INPROJ_Z_CHUNKS = 8
VMEM_LIMIT = 56 * 1024 * 1024
LOG2E = math.log2(math.e)

F32 = jnp.float32
BF16 = jnp.bfloat16
NT_DIMS = (((1,), (1,)), ((), ()))


def _rms(x, g):
    return x * lax.rsqrt(jnp.mean(x * x, axis=-1, keepdims=True) + NORM_EPS) * g


def _dot(a, b):
    return jnp.dot(a, b, preferred_element_type=F32)


def _resident(shape):
    nd = len(shape)
    return pl.BlockSpec(shape, lambda *_: (0,) * nd, pipeline_mode=pl.Buffered(1))


def _ffn_kernel(h_ref, pre_g_ref, wg_ref, wu_ref, wd_ref, post_g_ref, o_ref, act_ref, *, ff_chunk):
    x = h_ref[...]
    xn = _rms(x, pre_g_ref[...]).astype(BF16)
    d_ff = wg_ref.shape[1]
    for c in range(d_ff // ff_chunk):
        sl = slice(c * ff_chunk, (c + 1) * ff_chunk)
        g = _dot(xn, wg_ref[:, sl])
        u = _dot(xn, wu_ref[:, sl])
        act_ref[:, sl] = (g * jax.nn.sigmoid(g) * u).astype(BF16)
    y = _dot(act_ref[...], wd_ref[...])
    o_ref[...] = x + FFN_RES_WEIGHT * _rms(y, post_g_ref[...])


def _ffn(h, pre_g, wg, wu, wd, post_g, *, tm, ff_chunk):
    t, d = h.shape
    d_ff = wg.shape[1]
    return pl.pallas_call(
        functools.partial(_ffn_kernel, ff_chunk=ff_chunk),
        grid=(t // tm,),
        in_specs=[
            pl.BlockSpec((tm, d), lambda i: (i, 0)),
            _resident((1, d)),
            _resident((d, d_ff)),
            _resident((d, d_ff)),
            _resident((d_ff, d)),
            _resident((1, d)),
        ],
        out_specs=pl.BlockSpec((tm, d), lambda i: (i, 0)),
        out_shape=jax.ShapeDtypeStruct((t, d), F32),
        scratch_shapes=[pltpu.VMEM((tm, d_ff), BF16)],
        compiler_params=pltpu.CompilerParams(
            dimension_semantics=("arbitrary",), vmem_limit_bytes=VMEM_LIMIT),
        name="ffn",
    )(h, pre_g, wg, wu, wd, post_g)


def _inproj_kernel(h_ref, pos_col_ref, pos_row_ref, g_ref, wlat_ref, wrest_ref, qg_ref, wuqt_ref, kvg_ref,
                   wk_ref, wvt_ref, vones_ref, rope_ref, freq_col_ref, qt_ref, k_ref, vt_ref, z_ref, *,
                   q_scale):
    u = _rms(h_ref[0], g_ref[...]).astype(BF16)
    zc = wrest_ref.shape[1] // INPROJ_Z_CHUNKS

    def z_chunk(c):
        z_ref[0, :, c * zc:(c + 1) * zc] = _dot(u, wrest_ref[:, c * zc:(c + 1) * zc])

    half = QK_ROPE // 2
    lat = _dot(u, wlat_ref[...])
    z_chunk(0)
    ang_t = freq_col_ref[...] * pos_row_ref[0].astype(F32)
    cos_t = jnp.cos(ang_t)
    sin_t = jnp.sin(ang_t)
    ang = pos_col_ref[0].astype(F32) * rope_ref[0:1, :]
    cos = jnp.cos(ang)
    sin = jnp.sin(ang)
    sin_lo = sin * rope_ref[1:2, :]
    sin_hi = sin * rope_ref[2:3, :]
    z_chunk(1)
    qn = _rms(lat[:, :Q_LORA], qg_ref[...]).astype(BF16)
    kvn = _rms(lat[:, Q_LORA:Q_LORA + KV_LORA], kvg_ref[...]).astype(BF16)
    k_rope = lat[:, Q_LORA + KV_LORA:]
    qt = lax.dot_general(wuqt_ref[...], qn, NT_DIMS, preferred_element_type=F32)
    z_chunk(2)
    k = _dot(kvn, wk_ref[...])
    vt = lax.dot_general(wvt_ref[...], kvn, NT_DIMS, preferred_element_type=F32) + vones_ref[...]
    z_chunk(3)

    n_heads = qt.shape[0] // HEAD_PAD
    for h in range(n_heads):
        base = h * HEAD_PAD
        x1 = qt[base + QK_NOPE:base + QK_NOPE + half]
        x2 = qt[base + QK_NOPE + half:base + QK_NOPE + QK_ROPE]
        tile = jnp.concatenate([qt[base:base + QK_NOPE], x1 * cos_t - x2 * sin_t, x2 * cos_t + x1 * sin_t,
                                qt[base + QK_NOPE + QK_ROPE:base + HEAD_PAD]], axis=0)
        qt_ref[0, base:base + HEAD_PAD, :] = (tile * q_scale).astype(BF16)
        if h == n_heads // 2 - 1:
            z_chunk(4)
    z_chunk(5)

    k_rope = (k_rope * cos + pltpu.roll(k_rope, HEAD_PAD - half, 1) * sin_lo
              + pltpu.roll(k_rope, half, 1) * sin_hi)
    for h in range(n_heads):
        sl = slice(h * HEAD_PAD, (h + 1) * HEAD_PAD)
        k_ref[0, :, sl] = (k[:, sl] + k_rope).astype(BF16)
    z_chunk(6)
    for c in range(vt_ref.shape[1]):
        vt_ref[0, c] = vt[:, c * ATTN_TK:(c + 1) * ATTN_TK].astype(BF16)
    z_chunk(7)


def _inproj(h, pos_col, pos_row, g, wlat, wrest, qg, wuqt, kvg, wk, wvt, vones, rope_consts, freq_col, *,
            tm, q_scale):
    b, s, d = h.shape
    hp = wuqt.shape[0]
    n_rest = wrest.shape[1]
    vrows = wvt.shape[0]
    tok = lambda w: pl.BlockSpec((1, tm, w), lambda bi, i: (bi, i, 0))
    consts = (g, wlat, wrest, qg, wuqt, kvg, wk, wvt, vones, rope_consts, freq_col)
    return pl.pallas_call(
        functools.partial(_inproj_kernel, q_scale=q_scale),
        grid=(b, s // tm),
        in_specs=[tok(d), tok(1), pl.BlockSpec((1, 1, tm), lambda bi, i: (bi, 0, i))]
                 + [_resident(c.shape) for c in consts],
        out_specs=[pl.BlockSpec((1, hp, tm), lambda bi, i: (bi, 0, i)), tok(hp),
                   pl.BlockSpec((1, tm // ATTN_TK, vrows, ATTN_TK), lambda bi, i: (bi, i, 0, 0)),
                   tok(n_rest)],
        out_shape=[
            jax.ShapeDtypeStruct((b, hp, s), BF16),
            jax.ShapeDtypeStruct((b, s, hp), BF16),
            jax.ShapeDtypeStruct((b, s // ATTN_TK, vrows, ATTN_TK), BF16),
            jax.ShapeDtypeStruct((b, s, n_rest), F32),
        ],
        compiler_params=pltpu.CompilerParams(
            dimension_semantics=("arbitrary", "arbitrary"), vmem_limit_bytes=VMEM_LIMIT),
        name="inproj",
    )(h, pos_col, pos_row, *consts)


def _attn_kernel(qt_ref, k_ref, vt_ref, o_ref, *scratch, tq):
    tk = ATTN_TK
    nh = ATTN_HEADS
    qi = pl.program_id(2)
    per_tile = tq // tk
    s_even, s_odd, acc_refs = scratch[:nh], scratch[nh:2 * nh], scratch[2 * nh:]
    qts = [qt_ref[0, h * HEAD_PAD:(h + 1) * HEAD_PAD, :] for h in range(nh)]

    def prefetch(h, blk, s_ref):
        kb = k_ref[0, pl.ds(pl.multiple_of(blk * tk, tk), tk), h * HEAD_PAD:(h + 1) * HEAD_PAD]
        s = _dot(kb, qts[h])
        s_ref[...] = s
        return jnp.max(s, axis=0, keepdims=True)

    def values_t(h, blk):
        return vt_ref[0, blk, h * V_ROWS:(h + 1) * V_ROWS, :]

    def consume(h, s, blk, bmax, m):
        m_new = jnp.maximum(m, bmax)
        alpha = jnp.exp2(m - m_new)
        p = jnp.exp2((s - m_new).astype(BF16))
        acc_refs[h][...] = alpha * acc_refs[h][...] + _dot(values_t(h, blk), p)
        return m_new

    def pair(u, carry):
        ms, bmax_even = carry
        blk = per_tile * u
        bmax_odd = [prefetch(h, blk + 1, s_odd[h]) for h in range(nh)]
        ms = [consume(h, s_even[h][...], blk, bmax_even[h], ms[h]) for h in range(nh)]
        bmax_even = [prefetch(h, blk + 2, s_even[h]) for h in range(nh)]
        ms = [consume(h, s_odd[h][...], blk + 1, bmax_odd[h], ms[h]) for h in range(nh)]
        return tuple(ms), tuple(bmax_even)

    def two_pairs(u2, carry):
        return pair(2 * u2 + 1, pair(2 * u2, carry))

    for h in range(nh):
        acc_refs[h][...] = jnp.zeros((V_ROWS, tq), F32)
    carry = (tuple(jnp.full((1, tq), -jnp.inf, F32) for _ in range(nh)),
             tuple(prefetch(h, 0, s_even[h]) for h in range(nh)))
    carry = lax.fori_loop(0, qi >> 1, two_pairs, carry)
    ms, _ = lax.cond((qi & 1) == 1, lambda c: pair(qi - 1, c), lambda c: c, carry)

    half = tq // 2
    causal = (lax.broadcasted_iota(jnp.int32, (tk, tq), 0) <= lax.broadcasted_iota(jnp.int32, (tk, tq), 1))
    first = per_tile * qi
    outs = []
    for h in range(nh):
        kb = k_ref[0, pl.ds(pl.multiple_of((first + 1) * tk, tk), tk), h * HEAD_PAD:(h + 1) * HEAD_PAD]
        s0 = jnp.where(causal, s_even[h][...], -jnp.inf)
        s1 = jnp.where(causal[:, :half], _dot(kb, qts[h][:, half:]), -jnp.inf)
        m_new = jnp.maximum(ms[h], jnp.max(s0, axis=0, keepdims=True))
        m_hi = jnp.maximum(m_new[:, half:], jnp.max(s1, axis=0, keepdims=True))
        m_new = jnp.concatenate([m_new[:, :half], m_hi], axis=1)
        p0 = jnp.exp2((s0 - m_new).astype(BF16))
        p1 = jnp.exp2((s1 - m_hi).astype(BF16))
        acc = jnp.exp2(ms[h] - m_new) * acc_refs[h][...] + _dot(values_t(h, first), p0)
        acc = jnp.concatenate([acc[:, :half], acc[:, half:] + _dot(values_t(h, first + 1), p1)], axis=1)
        outs.append(acc[:V_HEAD] / acc[V_HEAD:V_HEAD + 1])
    o_ref[0] = jnp.concatenate(outs, axis=0).T.astype(BF16)


def _attn(qt, k, vt, *, tq):
    b, s, hp = k.shape
    nh = ATTN_HEADS
    groups = hp // (nh * HEAD_PAD)
    assert tq == 2 * ATTN_TK, "the key loop consumes blocks in pairs"
    return pl.pallas_call(
        functools.partial(_attn_kernel, tq=tq),
        grid=(b, groups, s // tq),
        in_specs=[
            pl.BlockSpec((1, nh * HEAD_PAD, tq), lambda bi, g, i: (bi, g, i)),
            pl.BlockSpec((1, s, nh * HEAD_PAD), lambda bi, g, i: (bi, 0, g)),
            pl.BlockSpec((1, s // ATTN_TK, nh * V_ROWS, ATTN_TK), lambda bi, g, i: (bi, 0, g, 0)),
        ],
        out_specs=pl.BlockSpec((1, tq, nh * V_HEAD), lambda bi, g, i: (bi, i, g)),
        out_shape=jax.ShapeDtypeStruct((b, s, groups * nh * V_HEAD), BF16),
        scratch_shapes=([pltpu.VMEM((ATTN_TK, tq), F32)] * (2 * nh) + [pltpu.VMEM((V_ROWS, tq), F32)] * nh),
        compiler_params=pltpu.CompilerParams(
            dimension_semantics=("arbitrary", "arbitrary", "arbitrary"), vmem_limit_bytes=VMEM_LIMIT),
        name="attn",
    )(qt, k, vt)


def _lru_kernel(xb_ref, yb_ref, cw_ref, cb_ref, wrg_ref, brg_ref, wig_ref, big_ref, lam_ref, o_ref,
                xs, ys, outs, a_s, b_s, xhalo, hcar, *, tc):
    t = pl.program_id(1)
    nch = SUBLANES
    steps = tc // nch
    pitch = steps + SUBLANES
    n_slabs = xs.shape[0]
    lanes = [slice(c * LANES, (c + 1) * LANES) for c in range(n_slabs)]

    @pl.when(t == 0)
    def _():
        xhalo[...] = jnp.zeros(xhalo.shape, F32)
        hcar[...] = jnp.zeros(hcar.shape, F32)

    for c in range(n_slabs):
        for j in range(nch):
            data = slice(j * pitch + SUBLANES, (j + 1) * pitch)
            xs[c, data, :] = xb_ref[0, j * steps:(j + 1) * steps, lanes[c]]
            ys[c, data, :] = jax.nn.gelu(yb_ref[0, j * steps:(j + 1) * steps, lanes[c]])
            if j == 0:
                xs[c, 0:SUBLANES, :] = xhalo[c]
            else:
                xs[c, j * pitch:j * pitch + SUBLANES, :] = xb_ref[0, j * steps - SUBLANES:j * steps, lanes[c]]
        xhalo[c] = xb_ref[0, tc - SUBLANES:tc, lanes[c]]

    def chain_rows(i):
        return pl.ds(pl.multiple_of(i * nch, nch), nch)

    for c in range(n_slabs):
        taps = [jnp.broadcast_to(cw_ref[kk:kk + 1, lanes[c]], (nch, LANES)) for kk in range(CONV_WIDTH)]
        bias = jnp.broadcast_to(cb_ref[:, lanes[c]], (nch, LANES))

        def conv_step(i, carry, c=c, taps=taps, bias=bias):
            xc = bias
            for kk in range(CONV_WIDTH):
                first = SUBLANES - (CONV_WIDTH - 1) + kk + i
                xc = xc + xs[c, pl.ds(first, nch, stride=pitch), :] * taps[kk]
            b_s[c, chain_rows(i), :] = xc
            return carry

        lax.fori_loop(0, steps, conv_step, 0, unroll=LRU_UNROLL)

    half_rate = (-0.5 * LRU_C * LOG2E) * jax.nn.softplus(-lam_ref[...])
    for c in range(n_slabs):
        xg = b_s[c]
        xg16 = xg.astype(BF16)
        t_r = jnp.tanh(_dot(xg16, wrg_ref[c]) + brg_ref[:, lanes[c]])
        t_i = jnp.tanh(_dot(xg16, wig_ref[c]) + big_ref[:, lanes[c]])
        a = jnp.exp2(half_rate[:, lanes[c]] * t_r + half_rate[:, lanes[c]])
        a_s[c] = a
        b_s[c] = jnp.sqrt(1.0 - a * a) * ((0.5 * t_i + 0.5) * xg)

    def scan_step(i, carry):
        hs, decays = carry
        new_h, new_decay = [], []
        for c in range(n_slabs):
            a = a_s[c, chain_rows(i), :]
            h = a * hs[c] + b_s[c, chain_rows(i), :]
            decay = a * decays[c]
            b_s[c, chain_rows(i), :] = h
            a_s[c, chain_rows(i), :] = decay
            new_h.append(h)
            new_decay.append(decay)
        return tuple(new_h), tuple(new_decay)

    zeros = tuple(jnp.zeros((nch, LANES), F32) for _ in range(n_slabs))
    ones = tuple(jnp.ones((nch, LANES), F32) for _ in range(n_slabs))
    h_end, decay_end = lax.fori_loop(0, steps, scan_step, (zeros, ones), unroll=LRU_UNROLL)

    carry_in = []
    for c in range(n_slabs):
        cur = hcar[c, 0:1, :]
        rows = [cur]
        for j in range(1, nch):
            cur = h_end[c][j - 1:j, :] + decay_end[c][j - 1:j, :] * cur
            rows.append(cur)
        carry_in.append(jnp.concatenate(rows, axis=0))
        hcar[c] = jnp.broadcast_to(h_end[c][nch - 1:nch, :] + decay_end[c][nch - 1:nch, :] * cur, (nch, LANES))

    for c in range(n_slabs):
        def out_step(i, carry, c=c):
            h = b_s[c, chain_rows(i), :] + a_s[c, chain_rows(i), :] * carry_in[c]
            gated = pl.ds(SUBLANES + i, nch, stride=pitch)
            outs[c, gated, :] = h * ys[c, gated, :]
            return carry

        lax.fori_loop(0, steps, out_step, 0, unroll=LRU_UNROLL)
    for c in range(n_slabs):
        for j in range(nch):
            o_ref[0, j * steps:(j + 1) * steps, lanes[c]] = (
                outs[c, j * pitch + SUBLANES:(j + 1) * pitch, :].astype(BF16))


def _lru(z, cw, cb, wrg, brg, wig, big, lam, *, tc):
    b, s, _ = z.shape
    w = cw.shape[1]
    n_slabs = w // LANES
    assert n_slabs == wrg.shape[0] and tc % (SUBLANES * SUBLANES) == 0
    chain_rows = tc + SUBLANES * SUBLANES
    slab = lambda rows: pltpu.VMEM((n_slabs, rows, LANES), F32)
    return pl.pallas_call(
        functools.partial(_lru_kernel, tc=tc),
        grid=(b, s // tc),
        in_specs=[
            pl.BlockSpec((1, tc, w), lambda bi, i: (bi, i, 0)),
            pl.BlockSpec((1, tc, w), lambda bi, i: (bi, i, 1)),
            _resident(cw.shape), _resident(cb.shape), _resident(wrg.shape), _resident(brg.shape),
            _resident(wig.shape), _resident(big.shape), _resident(lam.shape),
        ],
        out_specs=pl.BlockSpec((1, tc, w), lambda bi, i: (bi, i, 0)),
        out_shape=jax.ShapeDtypeStruct((b, s, w), BF16),
        scratch_shapes=[slab(chain_rows), slab(chain_rows), slab(chain_rows), slab(tc), slab(tc),
                        slab(SUBLANES), slab(SUBLANES)],
        compiler_params=pltpu.CompilerParams(
            dimension_semantics=("arbitrary", "arbitrary"), vmem_limit_bytes=VMEM_LIMIT),
        name="lru",
    )(z, z, cw, cb, wrg, brg, wig, big, lam)


def _merge_kernel(o_ref, lg_ref, ga_ref, gb_ref, h_ref, womla_ref, wolru_ref, wout_ref, g_ref, out_ref):
    o_mla = _dot(o_ref[0], womla_ref[...])
    o_lru = _dot(lg_ref[0], wolru_ref[...])
    merged = jax.nn.sigmoid(ga_ref[0]) * o_mla + jax.nn.sigmoid(gb_ref[0]) * o_lru
    y = _dot(merged.astype(BF16), wout_ref[...])
    out_ref[0] = h_ref[0] + _rms(y, g_ref[...])


def _merge(o, lg, z, h, womla, wolru, wout, g, *, tm):
    b, s, d = h.shape
    tok = lambda col: pl.BlockSpec((1, tm, d), lambda bi, i: (bi, i, col))
    return pl.pallas_call(
        _merge_kernel,
        grid=(b, s // tm),
        in_specs=[
            pl.BlockSpec((1, tm, o.shape[2]), lambda bi, i: (bi, i, 0)),
            tok(0), tok(2), tok(3), tok(0),
            _resident(womla.shape), _resident(wolru.shape), _resident(wout.shape), _resident(g.shape),
        ],
        out_specs=tok(0),
        out_shape=jax.ShapeDtypeStruct((b, s, d), F32),
        compiler_params=pltpu.CompilerParams(
            dimension_semantics=("arbitrary", "arbitrary"), vmem_limit_bytes=VMEM_LIMIT),
        name="merge",
    )(o, lg, z, z, h, womla, wolru, wout, g)


def _pad_heads(w, per_head):
    k = w.shape[0]
    w = w.reshape(k, MLA_HEADS, per_head)
    return jnp.pad(w, ((0, 0), (0, 0), (0, HEAD_PAD - per_head))).reshape(k, MLA_HEADS * HEAD_PAD)


def _rope_consts():
    half = QK_ROPE // 2
    inv_freq = ROPE_THETA ** (-jnp.arange(half, dtype=F32) / half)
    lane = jnp.arange(HEAD_PAD)
    lo = (lane >= QK_NOPE) & (lane < QK_NOPE + half)
    hi = (lane >= QK_NOPE + half) & (lane < QK_NOPE + QK_ROPE)
    freq = jnp.where(lo | hi, inv_freq[(lane - QK_NOPE) % half], 0.0)
    rows = jnp.stack([freq, jnp.where(lo, -1.0, 0.0), jnp.where(hi, 1.0, 0.0)]).astype(F32)
    return jnp.pad(rows, ((0, SUBLANES - rows.shape[0]), (0, 0)))


def kernel(x, positions, ffn1_pre_g, ffn1_w_gate, ffn1_w_up, ffn1_w_down, ffn1_post_g, mix_pre_g, w_in, q_norm_g, w_uq, kv_norm_g, w_ukv, w_o_mla, conv_w, conv_b, w_rg, b_rg, w_ig, b_ig, lru_lambda, w_o_lru, w_out, mix_post_g, ffn2_pre_g, ffn2_w_gate, ffn2_w_up, ffn2_w_down, ffn2_post_g):
    b, s, d = x.shape
    depth = w_in.shape[0]
    tm = min(512, s)
    tq = 2 * ATTN_TK
    tc = min(512, s)
    ff_chunk = ffn1_w_gate.shape[2] // 2
    q_scale = (QK_NOPE + QK_ROPE) ** -0.5 * LOG2E
    n_lat = Q_LORA + KV_LORA
    rope_consts = _rope_consts()
    vones = jnp.tile((jnp.arange(V_ROWS) == V_HEAD).astype(F32), MLA_HEADS).reshape(-1, 1)
    pos_col = positions.reshape(b, s, 1)
    pos_row = positions.reshape(b, 1, s)
    half = QK_ROPE // 2
    freq_col = (ROPE_THETA ** (-jnp.arange(half, dtype=F32) / half)).reshape(half, 1)
    row = lambda v: v.reshape(1, -1)

    h = x.reshape(b * s, d)
    for l in range(depth):
        h = _ffn(h, row(ffn1_pre_g[l]), ffn1_w_gate[l].astype(BF16), ffn1_w_up[l].astype(BF16),
                 ffn1_w_down[l].astype(BF16), row(ffn1_post_g[l]), tm=tm, ff_chunk=ff_chunk)

        wi = w_in[l]
        w_krope = jnp.pad(wi[:, n_lat:n_lat + QK_ROPE], ((0, 0), (QK_NOPE, HEAD_PAD - QK_NOPE - QK_ROPE)))
        wlat = jnp.concatenate([wi[:, :n_lat], w_krope], axis=1).astype(BF16)
        wrest = wi[:, n_lat + QK_ROPE:].astype(BF16)
        wuqt = _pad_heads(w_uq[l], QK_NOPE + QK_ROPE).T.astype(BF16)
        wkv = w_ukv[l].reshape(KV_LORA, MLA_HEADS, QK_NOPE + V_HEAD)
        wk = _pad_heads(wkv[:, :, :QK_NOPE].reshape(KV_LORA, -1), QK_NOPE).astype(BF16)
        wvt = jnp.pad(wkv[:, :, QK_NOPE:], ((0, 0), (0, 0), (0, V_ROWS - V_HEAD)))
        wvt = wvt.reshape(KV_LORA, MLA_HEADS * V_ROWS).T.astype(BF16)
        h3 = h.reshape(b, s, d)
        qt, k, vt, z = _inproj(h3, pos_col, pos_row, row(mix_pre_g[l]), wlat, wrest, row(q_norm_g[l]), wuqt,
                               row(kv_norm_g[l]), wk, wvt, vones, rope_consts, freq_col,
                               tm=tm, q_scale=q_scale)

        o = _attn(qt, k, vt, tq=tq)
        lg = _lru(z, conv_w[l], row(conv_b[l]), (0.5 * w_rg[l]).astype(BF16), row(0.5 * b_rg[l]),
                  (0.5 * w_ig[l]).astype(BF16), row(0.5 * b_ig[l]), row(lru_lambda[l]), tc=tc)

        h = _merge(o, lg, z, h3, w_o_mla[l].astype(BF16), w_o_lru[l].astype(BF16), w_out[l].astype(BF16),
                   row(mix_post_g[l]), tm=tm).reshape(b * s, d)

        h = _ffn(h, row(ffn2_pre_g[l]), ffn2_w_gate[l].astype(BF16), ffn2_w_up[l].astype(BF16),
                 ffn2_w_down[l].astype(BF16), row(ffn2_post_g[l]), tm=tm, ff_chunk=ff_chunk)
    return h.reshape(b, s, d)
```

```python
import functools
import math

import jax
import jax.numpy as jnp
from jax import lax
from jax.experimental import pallas as pl
from jax.experimental.pallas import tpu as pltpu

MLA_HEADS = 8
QK_NOPE = 64
QK_ROPE = 32
V_HEAD = 64
Q_LORA = 384
KV_LORA = 256
ROPE_THETA = 10000.0
LRU_BLOCKS = 8
CONV_WIDTH = 4
LRU_C = 8.0
FFN_RES_WEIGHT = 0.5
NORM_EPS = 1e-6

LANES = 128
SUBLANES = 8
BF16_ROWS = 16
HEAD_PAD = LANES
V_ROWS = V_HEAD + BF16_ROWS
SHIFT_LANE = QK_NOPE + QK_ROPE
ATTN_TK = 256
ATTN_HEADS = 2
LRU_UNROLL = 16
INPROJ_Z_CHUNKS = 8
VMEM_LIMIT = 56 * 1024 * 1024
LOG2E = math.log2(math.e)

F32 = jnp.float32
BF16 = jnp.bfloat16
NT_DIMS = (((1,), (1,)), ((), ()))


def _rms(x, g):
    return x * lax.rsqrt(jnp.mean(x * x, axis=-1, keepdims=True) + NORM_EPS) * g


def _dot(a, b):
    return jnp.dot(a, b, preferred_element_type=F32)


def _resident(shape):
    nd = len(shape)
    return pl.BlockSpec(shape, lambda *_: (0,) * nd, pipeline_mode=pl.Buffered(1))


def _ffn_kernel(h_ref, pre_g_ref, wg_ref, wu_ref, wd_ref, post_g_ref, o_ref, act_ref, *, ff_chunk):
    x = h_ref[...]
    xn = _rms(x, pre_g_ref[...]).astype(BF16)
    d_ff = wg_ref.shape[1]
    for c in range(d_ff // ff_chunk):
        sl = slice(c * ff_chunk, (c + 1) * ff_chunk)
        g = _dot(xn, wg_ref[:, sl])
        u = _dot(xn, wu_ref[:, sl])
        act_ref[:, sl] = (g * jax.nn.sigmoid(g) * u).astype(BF16)
    y = _dot(act_ref[...], wd_ref[...])
    o_ref[...] = x + FFN_RES_WEIGHT * _rms(y, post_g_ref[...])


def _ffn(h, pre_g, wg, wu, wd, post_g, *, tm, ff_chunk):
    t, d = h.shape
    d_ff = wg.shape[1]
    return pl.pallas_call(
        functools.partial(_ffn_kernel, ff_chunk=ff_chunk),
        grid=(t // tm,),
        in_specs=[
            pl.BlockSpec((tm, d), lambda i: (i, 0)),
            _resident((1, d)),
            _resident((d, d_ff)),
            _resident((d, d_ff)),
            _resident((d_ff, d)),
            _resident((1, d)),
        ],
        out_specs=pl.BlockSpec((tm, d), lambda i: (i, 0)),
        out_shape=jax.ShapeDtypeStruct((t, d), F32),
        scratch_shapes=[pltpu.VMEM((tm, d_ff), BF16)],
        compiler_params=pltpu.CompilerParams(
            dimension_semantics=("arbitrary",), vmem_limit_bytes=VMEM_LIMIT),
        name="ffn",
    )(h, pre_g, wg, wu, wd, post_g)


def _inproj_kernel(h_ref, pos_col_ref, pos_row_ref, g_ref, wlat_ref, wrest_ref, qg_ref, wuqt_ref, kvg_ref,
                   wk_ref, wvt_ref, vones_ref, rope_ref, freq_col_ref, qt_ref, k_ref, vt_ref, z_ref, *,
                   q_scale):
    u = _rms(h_ref[0], g_ref[...]).astype(BF16)
    zc = wrest_ref.shape[1] // INPROJ_Z_CHUNKS

    def z_chunk(c):
        z_ref[0, :, c * zc:(c + 1) * zc] = _dot(u, wrest_ref[:, c * zc:(c + 1) * zc])

    half = QK_ROPE // 2
    lat = _dot(u, wlat_ref[...])
    z_chunk(0)
    ang_t = freq_col_ref[...] * pos_row_ref[0].astype(F32)
    cos_t = jnp.cos(ang_t)
    sin_t = jnp.sin(ang_t)
    ang = pos_col_ref[0].astype(F32) * rope_ref[0:1, :]
    cos = jnp.cos(ang)
    sin = jnp.sin(ang)
    sin_lo = sin * rope_ref[1:2, :]
    sin_hi = sin * rope_ref[2:3, :]
    z_chunk(1)
    qn = _rms(lat[:, :Q_LORA], qg_ref[...]).astype(BF16)
    kvn = _rms(lat[:, Q_LORA:Q_LORA + KV_LORA], kvg_ref[...]).astype(BF16)
    k_rope = lat[:, Q_LORA + KV_LORA:]
    qt = lax.dot_general(wuqt_ref[...], qn, NT_DIMS, preferred_element_type=F32)
    z_chunk(2)
    k = _dot(kvn, wk_ref[...])
    vt = lax.dot_general(wvt_ref[...], kvn, NT_DIMS, preferred_element_type=F32) + vones_ref[...]
    z_chunk(3)

    n_heads = qt.shape[0] // HEAD_PAD
    for h in range(n_heads):
        base = h * HEAD_PAD
        x1 = qt[base + QK_NOPE:base + QK_NOPE + half]
        x2 = qt[base + QK_NOPE + half:base + QK_NOPE + QK_ROPE]
        tile = jnp.concatenate([qt[base:base + QK_NOPE], x1 * cos_t - x2 * sin_t, x2 * cos_t + x1 * sin_t,
                                qt[base + QK_NOPE + QK_ROPE:base + HEAD_PAD]], axis=0)
        qt_ref[0, base:base + HEAD_PAD, :] = (tile * q_scale).astype(BF16)
        if h == n_heads // 2 - 1:
            z_chunk(4)
    z_chunk(5)

    k_rope = (k_rope * cos + pltpu.roll(k_rope, HEAD_PAD - half, 1) * sin_lo
              + pltpu.roll(k_rope, half, 1) * sin_hi) + rope_ref[3:4, :]
    for h in range(n_heads):
        sl = slice(h * HEAD_PAD, (h + 1) * HEAD_PAD)
        k_ref[0, :, sl] = (k[:, sl] + k_rope).astype(BF16)
    z_chunk(6)
    for c in range(vt_ref.shape[1]):
        vt_ref[0, c] = vt[:, c * ATTN_TK:(c + 1) * ATTN_TK].astype(BF16)
    z_chunk(7)


def _inproj(h, pos_col, pos_row, g, wlat, wrest, qg, wuqt, kvg, wk, wvt, vones, rope_consts, freq_col, *,
            tm, q_scale):
    b, s, d = h.shape
    hp = wuqt.shape[0]
    n_rest = wrest.shape[1]
    vrows = wvt.shape[0]
    tok = lambda w: pl.BlockSpec((1, tm, w), lambda bi, i: (bi, i, 0))
    consts = (g, wlat, wrest, qg, wuqt, kvg, wk, wvt, vones, rope_consts, freq_col)
    return pl.pallas_call(
        functools.partial(_inproj_kernel, q_scale=q_scale),
        grid=(b, s // tm),
        in_specs=[tok(d), tok(1), pl.BlockSpec((1, 1, tm), lambda bi, i: (bi, 0, i))]
                 + [_resident(c.shape) for c in consts],
        out_specs=[pl.BlockSpec((1, hp, tm), lambda bi, i: (bi, 0, i)), tok(hp),
                   pl.BlockSpec((1, tm // ATTN_TK, vrows, ATTN_TK), lambda bi, i: (bi, i, 0, 0)),
                   tok(n_rest)],
        out_shape=[
            jax.ShapeDtypeStruct((b, hp, s), BF16),
            jax.ShapeDtypeStruct((b, s, hp), BF16),
            jax.ShapeDtypeStruct((b, s // ATTN_TK, vrows, ATTN_TK), BF16),
            jax.ShapeDtypeStruct((b, s, n_rest), F32),
        ],
        compiler_params=pltpu.CompilerParams(
            dimension_semantics=("arbitrary", "arbitrary"), vmem_limit_bytes=VMEM_LIMIT),
        name="inproj",
    )(h, pos_col, pos_row, *consts)


def _attn_kernel(qt_ref, k_ref, vt_ref, o_ref, *scratch, tq):
    tk = ATTN_TK
    nh = ATTN_HEADS
    qi = pl.program_id(2)
    per_tile = tq // tk
    half = tq // 2
    first = per_tile * qi
    acc_refs, p_even, p_odd = scratch[:nh], scratch[nh:2 * nh], scratch[2 * nh:]
    causal = (lax.broadcasted_iota(jnp.int32, (tk, tq), 0) <= lax.broadcasted_iota(jnp.int32, (tk, tq), 1))

    def keys(h, blk):
        return k_ref[0, pl.ds(pl.multiple_of(blk * tk, tk), tk), h * HEAD_PAD:(h + 1) * HEAD_PAD]

    def values_t(h, blk):
        return vt_ref[0, blk, h * V_ROWS:(h + 1) * V_ROWS, :]

    def finish(acc):
        return acc[:V_HEAD] / acc[V_HEAD:V_HEAD + 1]

    def write(outs):
        o_ref[0] = jnp.concatenate(outs, axis=0).T.astype(BF16)

    qts = []
    for h in range(nh):
        qt = qt_ref[0, h * HEAD_PAD:(h + 1) * HEAD_PAD, :]
        s0 = _dot(keys(h, 0), qt)
        s0 = jnp.where(jnp.logical_or(qi > 0, causal), s0, -jnp.inf)
        shift = jnp.max(s0, axis=0, keepdims=True)
        rows = lax.broadcasted_iota(jnp.int32, (BF16_ROWS, tq), 0)
        shift_rows = jnp.where(rows == 0, -shift, 0.0).astype(BF16)
        qts.append(jnp.concatenate([qt[:SHIFT_LANE], shift_rows, qt[SHIFT_LANE + BF16_ROWS:]], axis=0))
        acc_refs[h][...] = jnp.zeros((V_ROWS, tq), F32)

    def prefetch(h, blk, p_ref):
        p_ref[...] = jnp.exp2(_dot(keys(h, blk), qts[h])).astype(BF16)

    def consume(h, blk, p_ref):
        acc_refs[h][...] += _dot(values_t(h, blk), p_ref[...])

    def pair(u, carry):
        blk = per_tile * u
        for h in range(nh):
            prefetch(h, blk + 1, p_odd[h])
        for h in range(nh):
            consume(h, blk, p_even[h])
        for h in range(nh):
            prefetch(h, blk + 2, p_even[h])
        for h in range(nh):
            consume(h, blk + 1, p_odd[h])
        return carry

    def two_pairs(u2, carry):
        return pair(2 * u2 + 1, pair(2 * u2, carry))

    for h in range(nh):
        prefetch(h, 0, p_even[h])
    lax.fori_loop(0, qi >> 1, two_pairs, 0)
    lax.cond((qi & 1) == 1, lambda c: pair(qi - 1, c), lambda c: c, 0)

    accs = []
    for h in range(nh):
        p0 = jnp.where(causal, p_even[h][...], 0.0)
        s1 = jnp.where(causal[:, :half], _dot(keys(h, first + 1), qts[h][:, half:]), -jnp.inf)
        acc = acc_refs[h][...] + _dot(values_t(h, first), p0)
        pv1 = _dot(values_t(h, first + 1), jnp.exp2(s1).astype(BF16))
        accs.append(jnp.concatenate([acc[:, :half], acc[:, half:] + pv1], axis=1))
    write([finish(acc) for acc in accs])
    overflow = jnp.max(jnp.where(jnp.isfinite(jnp.concatenate(accs, axis=0)), 0.0, 1.0)) > 0.0

    @pl.when(overflow)
    def _():
        outs = []
        for h in range(nh):
            qt = qt_ref[0, h * HEAD_PAD:(h + 1) * HEAD_PAD, :]

            def update(s, blk, state, h=h):
                m, acc = state
                m_new = jnp.maximum(m, jnp.max(s, axis=0, keepdims=True))
                p = jnp.exp2(s - m_new).astype(BF16)
                return m_new, jnp.exp2(m - m_new) * acc + _dot(values_t(h, blk), p)

            def full_block(blk, state, h=h, qt=qt, update=update):
                return update(_dot(keys(h, blk), qt), blk, state)

            state = (jnp.full((1, tq), -jnp.inf, F32), jnp.zeros((V_ROWS, tq), F32))
            state = lax.fori_loop(0, first, full_block, state)
            for d in range(per_tile):
                s = jnp.where(lax.broadcasted_iota(jnp.int32, (tk, tq), 0) + d * tk
                              <= lax.broadcasted_iota(jnp.int32, (tk, tq), 1),
                              _dot(keys(h, first + d), qt), -jnp.inf)
                state = update(s, first + d, state)
            outs.append(finish(state[1]))
        write(outs)


def _attn(qt, k, vt, *, tq):
    b, s, hp = k.shape
    nh = ATTN_HEADS
    groups = hp // (nh * HEAD_PAD)
    assert tq == 2 * ATTN_TK, "the key loop consumes blocks in pairs"
    return pl.pallas_call(
        functools.partial(_attn_kernel, tq=tq),
        grid=(b, groups, s // tq),
        in_specs=[
            pl.BlockSpec((1, nh * HEAD_PAD, tq), lambda bi, g, i: (bi, g, i)),
            pl.BlockSpec((1, s, nh * HEAD_PAD), lambda bi, g, i: (bi, 0, g)),
            pl.BlockSpec((1, s // ATTN_TK, nh * V_ROWS, ATTN_TK), lambda bi, g, i: (bi, 0, g, 0)),
        ],
        out_specs=pl.BlockSpec((1, tq, nh * V_HEAD), lambda bi, g, i: (bi, i, g)),
        out_shape=jax.ShapeDtypeStruct((b, s, groups * nh * V_HEAD), BF16),
        scratch_shapes=([pltpu.VMEM((V_ROWS, tq), F32)] * nh + [pltpu.VMEM((ATTN_TK, tq), BF16)] * (2 * nh)),
        compiler_params=pltpu.CompilerParams(
            dimension_semantics=("arbitrary", "arbitrary", "arbitrary"), vmem_limit_bytes=VMEM_LIMIT),
        name="attn",
    )(qt, k, vt)


def _lru_kernel(xb_ref, yb_ref, cw_ref, cb_ref, wrg_ref, brg_ref, wig_ref, big_ref, lam_ref, o_ref,
                xs, ys, outs, a_s, b_s, xhalo, hcar, *, tc):
    t = pl.program_id(1)
    nch = SUBLANES
    steps = tc // nch
    pitch = steps + SUBLANES
    n_slabs = xs.shape[0]
    lanes = [slice(c * LANES, (c + 1) * LANES) for c in range(n_slabs)]

    @pl.when(t == 0)
    def _():
        xhalo[...] = jnp.zeros(xhalo.shape, F32)
        hcar[...] = jnp.zeros(hcar.shape, F32)

    for c in range(n_slabs):
        for j in range(nch):
            data = slice(j * pitch + SUBLANES, (j + 1) * pitch)
            xs[c, data, :] = xb_ref[0, j * steps:(j + 1) * steps, lanes[c]]
            ys[c, data, :] = jax.nn.gelu(yb_ref[0, j * steps:(j + 1) * steps, lanes[c]])
            if j == 0:
                xs[c, 0:SUBLANES, :] = xhalo[c]
            else:
                xs[c, j * pitch:j * pitch + SUBLANES, :] = xb_ref[0, j * steps - SUBLANES:j * steps, lanes[c]]
        xhalo[c] = xb_ref[0, tc - SUBLANES:tc, lanes[c]]

    def chain_rows(i):
        return pl.ds(pl.multiple_of(i * nch, nch), nch)

    for c in range(n_slabs):
        taps = [jnp.broadcast_to(cw_ref[kk:kk + 1, lanes[c]], (nch, LANES)) for kk in range(CONV_WIDTH)]
        bias = jnp.broadcast_to(cb_ref[:, lanes[c]], (nch, LANES))

        def conv_step(i, carry, c=c, taps=taps, bias=bias):
            xc = bias
            for kk in range(CONV_WIDTH):
                first = SUBLANES - (CONV_WIDTH - 1) + kk + i
                xc = xc + xs[c, pl.ds(first, nch, stride=pitch), :] * taps[kk]
            b_s[c, chain_rows(i), :] = xc
            return carry

        lax.fori_loop(0, steps, conv_step, 0, unroll=LRU_UNROLL)

    half_rate = (-0.5 * LRU_C * LOG2E) * jax.nn.softplus(-lam_ref[...])
    for c in range(n_slabs):
        xg = b_s[c]
        xg16 = xg.astype(BF16)
        t_r = jnp.tanh(_dot(xg16, wrg_ref[c]) + brg_ref[:, lanes[c]])
        t_i = jnp.tanh(_dot(xg16, wig_ref[c]) + big_ref[:, lanes[c]])
        a = jnp.exp2(half_rate[:, lanes[c]] * t_r + half_rate[:, lanes[c]])
        a_s[c] = a
        b_s[c] = jnp.sqrt(1.0 - a * a) * ((0.5 * t_i + 0.5) * xg)

    def scan_step(i, carry):
        hs, decays = carry
        new_h, new_decay = [], []
        for c in range(n_slabs):
            a = a_s[c, chain_rows(i), :]
            h = a * hs[c] + b_s[c, chain_rows(i), :]
            decay = a * decays[c]
            b_s[c, chain_rows(i), :] = h
            a_s[c, chain_rows(i), :] = decay
            new_h.append(h)
            new_decay.append(decay)
        return tuple(new_h), tuple(new_decay)

    zeros = tuple(jnp.zeros((nch, LANES), F32) for _ in range(n_slabs))
    ones = tuple(jnp.ones((nch, LANES), F32) for _ in range(n_slabs))
    h_end, decay_end = lax.fori_loop(0, steps, scan_step, (zeros, ones), unroll=LRU_UNROLL)

    carry_in = []
    for c in range(n_slabs):
        cur = hcar[c, 0:1, :]
        rows = [cur]
        for j in range(1, nch):
            cur = h_end[c][j - 1:j, :] + decay_end[c][j - 1:j, :] * cur
            rows.append(cur)
        carry_in.append(jnp.concatenate(rows, axis=0))
        hcar[c] = jnp.broadcast_to(h_end[c][nch - 1:nch, :] + decay_end[c][nch - 1:nch, :] * cur, (nch, LANES))

    for c in range(n_slabs):
        def out_step(i, carry, c=c):
            h = b_s[c, chain_rows(i), :] + a_s[c, chain_rows(i), :] * carry_in[c]
            gated = pl.ds(SUBLANES + i, nch, stride=pitch)
            outs[c, gated, :] = h * ys[c, gated, :]
            return carry

        lax.fori_loop(0, steps, out_step, 0, unroll=LRU_UNROLL)
    for c in range(n_slabs):
        for j in range(nch):
            o_ref[0, j * steps:(j + 1) * steps, lanes[c]] = (
                outs[c, j * pitch + SUBLANES:(j + 1) * pitch, :].astype(BF16))


def _lru(z, cw, cb, wrg, brg, wig, big, lam, *, tc):
    b, s, _ = z.shape
    w = cw.shape[1]
    n_slabs = w // LANES
    assert n_slabs == wrg.shape[0] and tc % (SUBLANES * SUBLANES) == 0
    chain_rows = tc + SUBLANES * SUBLANES
    slab = lambda rows: pltpu.VMEM((n_slabs, rows, LANES), F32)
    return pl.pallas_call(
        functools.partial(_lru_kernel, tc=tc),
        grid=(b, s // tc),
        in_specs=[
            pl.BlockSpec((1, tc, w), lambda bi, i: (bi, i, 0)),
            pl.BlockSpec((1, tc, w), lambda bi, i: (bi, i, 1)),
            _resident(cw.shape), _resident(cb.shape), _resident(wrg.shape), _resident(brg.shape),
            _resident(wig.shape), _resident(big.shape), _resident(lam.shape),
        ],
        out_specs=pl.BlockSpec((1, tc, w), lambda bi, i: (bi, i, 0)),
        out_shape=jax.ShapeDtypeStruct((b, s, w), BF16),
        scratch_shapes=[slab(chain_rows), slab(chain_rows), slab(chain_rows), slab(tc), slab(tc),
                        slab(SUBLANES), slab(SUBLANES)],
        compiler_params=pltpu.CompilerParams(
            dimension_semantics=("arbitrary", "arbitrary"), vmem_limit_bytes=VMEM_LIMIT),
        name="lru",
    )(z, z, cw, cb, wrg, brg, wig, big, lam)


def _merge_kernel(o_ref, lg_ref, ga_ref, gb_ref, h_ref, womla_ref, wolru_ref, wout_ref, g_ref, out_ref):
    o_mla = _dot(o_ref[0], womla_ref[...])
    o_lru = _dot(lg_ref[0], wolru_ref[...])
    merged = jax.nn.sigmoid(ga_ref[0]) * o_mla + jax.nn.sigmoid(gb_ref[0]) * o_lru
    y = _dot(merged.astype(BF16), wout_ref[...])
    out_ref[0] = h_ref[0] + _rms(y, g_ref[...])


def _merge(o, lg, z, h, womla, wolru, wout, g, *, tm):
    b, s, d = h.shape
    tok = lambda col: pl.BlockSpec((1, tm, d), lambda bi, i: (bi, i, col))
    return pl.pallas_call(
        _merge_kernel,
        grid=(b, s // tm),
        in_specs=[
            pl.BlockSpec((1, tm, o.shape[2]), lambda bi, i: (bi, i, 0)),
            tok(0), tok(2), tok(3), tok(0),
            _resident(womla.shape), _resident(wolru.shape), _resident(wout.shape), _resident(g.shape),
        ],
        out_specs=tok(0),
        out_shape=jax.ShapeDtypeStruct((b, s, d), F32),
        compiler_params=pltpu.CompilerParams(
            dimension_semantics=("arbitrary", "arbitrary"), vmem_limit_bytes=VMEM_LIMIT),
        name="merge",
    )(o, lg, z, z, h, womla, wolru, wout, g)


def _pad_heads(w, per_head):
    k = w.shape[0]
    w = w.reshape(k, MLA_HEADS, per_head)
    return jnp.pad(w, ((0, 0), (0, 0), (0, HEAD_PAD - per_head))).reshape(k, MLA_HEADS * HEAD_PAD)


def _rope_consts():
    half = QK_ROPE // 2
    inv_freq = ROPE_THETA ** (-jnp.arange(half, dtype=F32) / half)
    lane = jnp.arange(HEAD_PAD)
    lo = (lane >= QK_NOPE) & (lane < QK_NOPE + half)
    hi = (lane >= QK_NOPE + half) & (lane < QK_NOPE + QK_ROPE)
    freq = jnp.where(lo | hi, inv_freq[(lane - QK_NOPE) % half], 0.0)
    rows = jnp.stack([freq, jnp.where(lo, -1.0, 0.0), jnp.where(hi, 1.0, 0.0),
                      jnp.where(lane == SHIFT_LANE, 1.0, 0.0)]).astype(F32)
    return jnp.pad(rows, ((0, SUBLANES - rows.shape[0]), (0, 0)))


def kernel(x, positions, ffn1_pre_g, ffn1_w_gate, ffn1_w_up, ffn1_w_down, ffn1_post_g, mix_pre_g, w_in, q_norm_g, w_uq, kv_norm_g, w_ukv, w_o_mla, conv_w, conv_b, w_rg, b_rg, w_ig, b_ig, lru_lambda, w_o_lru, w_out, mix_post_g, ffn2_pre_g, ffn2_w_gate, ffn2_w_up, ffn2_w_down, ffn2_post_g):
    b, s, d = x.shape
    depth = w_in.shape[0]
    tm = min(512, s)
    tq = 2 * ATTN_TK
    tc = min(512, s)
    ff_chunk = ffn1_w_gate.shape[2] // 2
    q_scale = (QK_NOPE + QK_ROPE) ** -0.5 * LOG2E
    n_lat = Q_LORA + KV_LORA
    rope_consts = _rope_consts()
    vones = jnp.tile((jnp.arange(V_ROWS) == V_HEAD).astype(F32), MLA_HEADS).reshape(-1, 1)
    pos_col = positions.reshape(b, s, 1)
    pos_row = positions.reshape(b, 1, s)
    half = QK_ROPE // 2
    freq_col = (ROPE_THETA ** (-jnp.arange(half, dtype=F32) / half)).reshape(half, 1)
    row = lambda v: v.reshape(1, -1)

    h = x.reshape(b * s, d)
    for l in range(depth):
        h = _ffn(h, row(ffn1_pre_g[l]), ffn1_w_gate[l].astype(BF16), ffn1_w_up[l].astype(BF16),
                 ffn1_w_down[l].astype(BF16), row(ffn1_post_g[l]), tm=tm, ff_chunk=ff_chunk)

        wi = w_in[l]
        w_krope = jnp.pad(wi[:, n_lat:n_lat + QK_ROPE], ((0, 0), (QK_NOPE, HEAD_PAD - QK_NOPE - QK_ROPE)))
        wlat = jnp.concatenate([wi[:, :n_lat], w_krope], axis=1).astype(BF16)
        wrest = wi[:, n_lat + QK_ROPE:].astype(BF16)
        wuqt = _pad_heads(w_uq[l], QK_NOPE + QK_ROPE).T.astype(BF16)
        wkv = w_ukv[l].reshape(KV_LORA, MLA_HEADS, QK_NOPE + V_HEAD)
        wk = _pad_heads(wkv[:, :, :QK_NOPE].reshape(KV_LORA, -1), QK_NOPE).astype(BF16)
        wvt = jnp.pad(wkv[:, :, QK_NOPE:], ((0, 0), (0, 0), (0, V_ROWS - V_HEAD)))
        wvt = wvt.reshape(KV_LORA, MLA_HEADS * V_ROWS).T.astype(BF16)
        h3 = h.reshape(b, s, d)
        qt, k, vt, z = _inproj(h3, pos_col, pos_row, row(mix_pre_g[l]), wlat, wrest, row(q_norm_g[l]), wuqt,
                               row(kv_norm_g[l]), wk, wvt, vones, rope_consts, freq_col,
                               tm=tm, q_scale=q_scale)

        o = _attn(qt, k, vt, tq=tq)
        lg = _lru(z, conv_w[l], row(conv_b[l]), (0.5 * w_rg[l]).astype(BF16), row(0.5 * b_rg[l]),
                  (0.5 * w_ig[l]).astype(BF16), row(0.5 * b_ig[l]), row(lru_lambda[l]), tc=tc)

        h = _merge(o, lg, z, h3, w_o_mla[l].astype(BF16), w_o_lru[l].astype(BF16), w_out[l].astype(BF16),
                   row(mix_post_g[l]), tm=tm).reshape(b * s, d)

        h = _ffn(h, row(ffn2_pre_g[l]), ffn2_w_gate[l].astype(BF16), ffn2_w_up[l].astype(BF16),
                 ffn2_w_down[l].astype(BF16), row(ffn2_post_g[l]), tm=tm, ff_chunk=ff_chunk)
    return h.reshape(b, s, d)
```

```python
import functools
import math

import jax
import jax.numpy as jnp
from jax import lax
from jax.experimental import pallas as pl
from jax.experimental.pallas import tpu as pltpu

MLA_HEADS = 8
QK_NOPE = 64
QK_ROPE = 32
V_HEAD = 64
Q_LORA = 384
KV_LORA = 256
ROPE_THETA = 10000.0
LRU_BLOCKS = 8
CONV_WIDTH = 4
LRU_C = 8.0
FFN_RES_WEIGHT = 0.5
NORM_EPS = 1e-6

LANES = 128
SUBLANES = 8
BF16_ROWS = 16
HEAD_PAD = LANES
V_ROWS = V_HEAD + BF16_ROWS
SHIFT_LANE = QK_NOPE + QK_ROPE
ATTN_TK = 256
ATTN_HEADS = 2
ATTN_PAIRS = 4
LRU_UNROLL = 16
INPROJ_Z_CHUNKS = 8
VMEM_LIMIT = 56 * 1024 * 1024
LOG2E = math.log2(math.e)

F32 = jnp.float32
BF16 = jnp.bfloat16
NT_DIMS = (((1,), (1,)), ((), ()))


def _rms(x, g):
    return x * lax.rsqrt(jnp.mean(x * x, axis=-1, keepdims=True) + NORM_EPS) * g


def _dot(a, b):
    return jnp.dot(a, b, preferred_element_type=F32)


def _resident(shape):
    nd = len(shape)
    return pl.BlockSpec(shape, lambda *_: (0,) * nd, pipeline_mode=pl.Buffered(1))


def _ffn_kernel(h_ref, pre_g_ref, wg_ref, wu_ref, wd_ref, post_g_ref, o_ref, act_ref, *, ff_chunk):
    x = h_ref[...]
    xn = _rms(x, pre_g_ref[...]).astype(BF16)
    d_ff = wg_ref.shape[1]
    for c in range(d_ff // ff_chunk):
        sl = slice(c * ff_chunk, (c + 1) * ff_chunk)
        g = _dot(xn, wg_ref[:, sl])
        u = _dot(xn, wu_ref[:, sl])
        act_ref[:, sl] = (g * jax.nn.sigmoid(g) * u).astype(BF16)
    y = _dot(act_ref[...], wd_ref[...])
    o_ref[...] = x + FFN_RES_WEIGHT * _rms(y, post_g_ref[...])


def _ffn(h, pre_g, wg, wu, wd, post_g, *, tm, ff_chunk):
    t, d = h.shape
    d_ff = wg.shape[1]
    return pl.pallas_call(
        functools.partial(_ffn_kernel, ff_chunk=ff_chunk),
        grid=(t // tm,),
        in_specs=[
            pl.BlockSpec((tm, d), lambda i: (i, 0)),
            _resident((1, d)),
            _resident((d, d_ff)),
            _resident((d, d_ff)),
            _resident((d_ff, d)),
            _resident((1, d)),
        ],
        out_specs=pl.BlockSpec((tm, d), lambda i: (i, 0)),
        out_shape=jax.ShapeDtypeStruct((t, d), F32),
        scratch_shapes=[pltpu.VMEM((tm, d_ff), BF16)],
        compiler_params=pltpu.CompilerParams(
            dimension_semantics=("arbitrary",), vmem_limit_bytes=VMEM_LIMIT),
        name="ffn",
    )(h, pre_g, wg, wu, wd, post_g)


def _inproj_kernel(h_ref, pos_col_ref, pos_row_ref, g_ref, wlat_ref, wrest_ref, qg_ref, wuqt_ref, kvg_ref,
                   wk_ref, wvt_ref, vones_ref, rope_ref, freq_col_ref, qt_ref, k_ref, vt_ref, z_ref, *,
                   q_scale):
    u = _rms(h_ref[0], g_ref[...]).astype(BF16)
    zc = wrest_ref.shape[1] // INPROJ_Z_CHUNKS

    def z_chunk(c):
        z_ref[0, :, c * zc:(c + 1) * zc] = _dot(u, wrest_ref[:, c * zc:(c + 1) * zc])

    half = QK_ROPE // 2
    lat = _dot(u, wlat_ref[...])
    z_chunk(0)
    ang_t = freq_col_ref[...] * pos_row_ref[0].astype(F32)
    cos_t = jnp.cos(ang_t)
    sin_t = jnp.sin(ang_t)
    ang = pos_col_ref[0].astype(F32) * rope_ref[0:1, :]
    cos = jnp.cos(ang)
    sin = jnp.sin(ang)
    sin_lo = sin * rope_ref[1:2, :]
    sin_hi = sin * rope_ref[2:3, :]
    z_chunk(1)
    qn = _rms(lat[:, :Q_LORA], qg_ref[...]).astype(BF16)
    kvn = _rms(lat[:, Q_LORA:Q_LORA + KV_LORA], kvg_ref[...]).astype(BF16)
    k_rope = lat[:, Q_LORA + KV_LORA:]
    qt = lax.dot_general(wuqt_ref[...], qn, NT_DIMS, preferred_element_type=F32)
    z_chunk(2)
    k = _dot(kvn, wk_ref[...])
    vt = lax.dot_general(wvt_ref[...], kvn, NT_DIMS, preferred_element_type=F32) + vones_ref[...]
    z_chunk(3)

    n_heads = qt.shape[0] // HEAD_PAD
    for h in range(n_heads):
        base = h * HEAD_PAD
        x1 = qt[base + QK_NOPE:base + QK_NOPE + half]
        x2 = qt[base + QK_NOPE + half:base + QK_NOPE + QK_ROPE]
        tile = jnp.concatenate([qt[base:base + QK_NOPE], x1 * cos_t - x2 * sin_t, x2 * cos_t + x1 * sin_t,
                                qt[base + QK_NOPE + QK_ROPE:base + HEAD_PAD]], axis=0)
        qt_ref[0, base:base + HEAD_PAD, :] = (tile * q_scale).astype(BF16)
        if h == n_heads // 2 - 1:
            z_chunk(4)
    z_chunk(5)

    k_rope = (k_rope * cos + pltpu.roll(k_rope, HEAD_PAD - half, 1) * sin_lo
              + pltpu.roll(k_rope, half, 1) * sin_hi) + rope_ref[3:4, :]
    for h in range(n_heads):
        sl = slice(h * HEAD_PAD, (h + 1) * HEAD_PAD)
        k_ref[0, :, sl] = (k[:, sl] + k_rope).astype(BF16)
    z_chunk(6)
    for c in range(vt_ref.shape[1]):
        vt_ref[0, c] = vt[:, c * ATTN_TK:(c + 1) * ATTN_TK].astype(BF16)
    z_chunk(7)


def _inproj(h, pos_col, pos_row, g, wlat, wrest, qg, wuqt, kvg, wk, wvt, vones, rope_consts, freq_col, *,
            tm, q_scale):
    b, s, d = h.shape
    hp = wuqt.shape[0]
    n_rest = wrest.shape[1]
    vrows = wvt.shape[0]
    tok = lambda w: pl.BlockSpec((1, tm, w), lambda bi, i: (bi, i, 0))
    consts = (g, wlat, wrest, qg, wuqt, kvg, wk, wvt, vones, rope_consts, freq_col)
    return pl.pallas_call(
        functools.partial(_inproj_kernel, q_scale=q_scale),
        grid=(b, s // tm),
        in_specs=[tok(d), tok(1), pl.BlockSpec((1, 1, tm), lambda bi, i: (bi, 0, i))]
                 + [_resident(c.shape) for c in consts],
        out_specs=[pl.BlockSpec((1, hp, tm), lambda bi, i: (bi, 0, i)), tok(hp),
                   pl.BlockSpec((1, tm // ATTN_TK, vrows, ATTN_TK), lambda bi, i: (bi, i, 0, 0)),
                   tok(n_rest)],
        out_shape=[
            jax.ShapeDtypeStruct((b, hp, s), BF16),
            jax.ShapeDtypeStruct((b, s, hp), BF16),
            jax.ShapeDtypeStruct((b, s // ATTN_TK, vrows, ATTN_TK), BF16),
            jax.ShapeDtypeStruct((b, s, n_rest), F32),
        ],
        compiler_params=pltpu.CompilerParams(
            dimension_semantics=("arbitrary", "arbitrary"), vmem_limit_bytes=VMEM_LIMIT),
        name="inproj",
    )(h, pos_col, pos_row, *consts)


def _attn_kernel(qt_ref, k_ref, vt_ref, o_ref, *scratch, tq):
    tk = ATTN_TK
    nh = ATTN_HEADS
    qi = pl.program_id(2)
    per_tile = tq // tk
    half = tq // 2
    first = per_tile * qi
    acc_refs, p_even, p_odd = scratch[:nh], scratch[nh:2 * nh], scratch[2 * nh:]
    causal = (lax.broadcasted_iota(jnp.int32, (tk, tq), 0) <= lax.broadcasted_iota(jnp.int32, (tk, tq), 1))

    def keys(h, blk):
        return k_ref[0, pl.ds(pl.multiple_of(blk * tk, tk), tk), h * HEAD_PAD:(h + 1) * HEAD_PAD]

    def values_t(h, blk):
        return vt_ref[0, blk, h * V_ROWS:(h + 1) * V_ROWS, :]

    def finish(acc):
        return acc[:V_HEAD] / acc[V_HEAD:V_HEAD + 1]

    def write(outs):
        o_ref[0] = jnp.concatenate(outs, axis=0).T.astype(BF16)

    qts = []
    for h in range(nh):
        qt = qt_ref[0, h * HEAD_PAD:(h + 1) * HEAD_PAD, :]
        s0 = _dot(keys(h, 0), qt)
        shift = jnp.max(jnp.where(jnp.logical_or(qi > 0, causal), s0, -jnp.inf), axis=0, keepdims=True)
        p_even[h][...] = jnp.exp2(s0 - shift).astype(BF16)
        rows = lax.broadcasted_iota(jnp.int32, (BF16_ROWS, tq), 0)
        shift_rows = jnp.where(rows == 0, -shift, 0.0).astype(BF16)
        qts.append(jnp.concatenate([qt[:SHIFT_LANE], shift_rows, qt[SHIFT_LANE + BF16_ROWS:]], axis=0))
        acc_refs[h][...] = jnp.zeros((V_ROWS, tq), F32)

    def prefetch(h, blk, p_ref):
        p_ref[...] = jnp.exp2(_dot(keys(h, blk), qts[h])).astype(BF16)

    def consume(h, blk, p_ref):
        acc_refs[h][...] += _dot(values_t(h, blk), p_ref[...])

    def pair(u, carry):
        blk = per_tile * u
        for h in range(nh):
            prefetch(h, blk + 1, p_odd[h])
        for h in range(nh):
            consume(h, blk, p_even[h])
        for h in range(nh):
            prefetch(h, blk + 2, p_even[h])
        for h in range(nh):
            consume(h, blk + 1, p_odd[h])
        return carry

    def pair_group(ug, carry):
        for j in range(ATTN_PAIRS):
            carry = pair(ATTN_PAIRS * ug + j, carry)
        return carry

    n_groups = qi >> (ATTN_PAIRS.bit_length() - 1)
    lax.fori_loop(0, n_groups, pair_group, 0)
    lax.fori_loop(n_groups * ATTN_PAIRS, qi, pair, 0)

    accs = []
    for h in range(nh):
        p0 = jnp.where(causal, p_even[h][...], 0.0)
        s1 = jnp.where(causal[:, :half], _dot(keys(h, first + 1), qts[h][:, half:]), -jnp.inf)
        acc = acc_refs[h][...] + _dot(values_t(h, first), p0)
        pv1 = _dot(values_t(h, first + 1), jnp.exp2(s1).astype(BF16))
        accs.append(jnp.concatenate([acc[:, :half], acc[:, half:] + pv1], axis=1))
    write([finish(acc) for acc in accs])
    overflow = jnp.max(jnp.where(jnp.isfinite(jnp.concatenate(accs, axis=0)), 0.0, 1.0)) > 0.0

    @pl.when(overflow)
    def _():
        outs = []
        for h in range(nh):
            qt = qt_ref[0, h * HEAD_PAD:(h + 1) * HEAD_PAD, :]

            def update(s, blk, state, h=h):
                m, acc = state
                m_new = jnp.maximum(m, jnp.max(s, axis=0, keepdims=True))
                p = jnp.exp2(s - m_new).astype(BF16)
                return m_new, jnp.exp2(m - m_new) * acc + _dot(values_t(h, blk), p)

            def full_block(blk, state, h=h, qt=qt, update=update):
                return update(_dot(keys(h, blk), qt), blk, state)

            state = (jnp.full((1, tq), -jnp.inf, F32), jnp.zeros((V_ROWS, tq), F32))
            state = lax.fori_loop(0, first, full_block, state)
            for d in range(per_tile):
                s = jnp.where(lax.broadcasted_iota(jnp.int32, (tk, tq), 0) + d * tk
                              <= lax.broadcasted_iota(jnp.int32, (tk, tq), 1),
                              _dot(keys(h, first + d), qt), -jnp.inf)
                state = update(s, first + d, state)
            outs.append(finish(state[1]))
        write(outs)


def _attn(qt, k, vt, *, tq):
    b, s, hp = k.shape
    nh = ATTN_HEADS
    groups = hp // (nh * HEAD_PAD)
    assert tq == 2 * ATTN_TK, "the key loop consumes blocks in pairs"
    return pl.pallas_call(
        functools.partial(_attn_kernel, tq=tq),
        grid=(b, groups, s // tq),
        in_specs=[
            pl.BlockSpec((1, nh * HEAD_PAD, tq), lambda bi, g, i: (bi, g, i)),
            pl.BlockSpec((1, s, nh * HEAD_PAD), lambda bi, g, i: (bi, 0, g)),
            pl.BlockSpec((1, s // ATTN_TK, nh * V_ROWS, ATTN_TK), lambda bi, g, i: (bi, 0, g, 0)),
        ],
        out_specs=pl.BlockSpec((1, tq, nh * V_HEAD), lambda bi, g, i: (bi, i, g)),
        out_shape=jax.ShapeDtypeStruct((b, s, groups * nh * V_HEAD), BF16),
        scratch_shapes=([pltpu.VMEM((V_ROWS, tq), F32)] * nh + [pltpu.VMEM((ATTN_TK, tq), BF16)] * (2 * nh)),
        compiler_params=pltpu.CompilerParams(
            dimension_semantics=("arbitrary", "arbitrary", "arbitrary"), vmem_limit_bytes=VMEM_LIMIT),
        name="attn",
    )(qt, k, vt)


def _lru_kernel(xb_ref, yb_ref, cw_ref, cb_ref, wrg_ref, brg_ref, wig_ref, big_ref, lam_ref, o_ref,
                xs, ys, outs, a_s, b_s, xhalo, hcar, *, tc):
    t = pl.program_id(1)
    nch = SUBLANES
    steps = tc // nch
    pitch = steps + SUBLANES
    n_slabs = xs.shape[0]
    lanes = [slice(c * LANES, (c + 1) * LANES) for c in range(n_slabs)]

    @pl.when(t == 0)
    def _():
        xhalo[...] = jnp.zeros(xhalo.shape, F32)
        hcar[...] = jnp.zeros(hcar.shape, F32)

    for c in range(n_slabs):
        for j in range(nch):
            data = slice(j * pitch + SUBLANES, (j + 1) * pitch)
            xs[c, data, :] = xb_ref[0, j * steps:(j + 1) * steps, lanes[c]]
            ys[c, data, :] = jax.nn.gelu(yb_ref[0, j * steps:(j + 1) * steps, lanes[c]])
            if j == 0:
                xs[c, 0:SUBLANES, :] = xhalo[c]
            else:
                xs[c, j * pitch:j * pitch + SUBLANES, :] = xb_ref[0, j * steps - SUBLANES:j * steps, lanes[c]]
        xhalo[c] = xb_ref[0, tc - SUBLANES:tc, lanes[c]]

    def chain_rows(i):
        return pl.ds(pl.multiple_of(i * nch, nch), nch)

    for c in range(n_slabs):
        taps = [jnp.broadcast_to(cw_ref[kk:kk + 1, lanes[c]], (nch, LANES)) for kk in range(CONV_WIDTH)]
        bias = jnp.broadcast_to(cb_ref[:, lanes[c]], (nch, LANES))

        def conv_step(i, carry, c=c, taps=taps, bias=bias):
            xc = bias
            for kk in range(CONV_WIDTH):
                first = SUBLANES - (CONV_WIDTH - 1) + kk + i
                xc = xc + xs[c, pl.ds(first, nch, stride=pitch), :] * taps[kk]
            b_s[c, chain_rows(i), :] = xc
            return carry

        lax.fori_loop(0, steps, conv_step, 0, unroll=LRU_UNROLL)

    half_rate = (-0.5 * LRU_C * LOG2E) * jax.nn.softplus(-lam_ref[...])
    for c in range(n_slabs):
        xg = b_s[c]
        xg16 = xg.astype(BF16)
        t_r = jnp.tanh(_dot(xg16, wrg_ref[c]) + brg_ref[:, lanes[c]])
        t_i = jnp.tanh(_dot(xg16, wig_ref[c]) + big_ref[:, lanes[c]])
        a = jnp.exp2(half_rate[:, lanes[c]] * t_r + half_rate[:, lanes[c]])
        a_s[c] = a
        b_s[c] = jnp.sqrt(1.0 - a * a) * ((0.5 * t_i + 0.5) * xg)

    def scan_step(i, carry):
        hs, decays = carry
        new_h, new_decay = [], []
        for c in range(n_slabs):
            a = a_s[c, chain_rows(i), :]
            h = a * hs[c] + b_s[c, chain_rows(i), :]
            decay = a * decays[c]
            b_s[c, chain_rows(i), :] = h
            a_s[c, chain_rows(i), :] = decay
            new_h.append(h)
            new_decay.append(decay)
        return tuple(new_h), tuple(new_decay)

    zeros = tuple(jnp.zeros((nch, LANES), F32) for _ in range(n_slabs))
    ones = tuple(jnp.ones((nch, LANES), F32) for _ in range(n_slabs))
    h_end, decay_end = lax.fori_loop(0, steps, scan_step, (zeros, ones), unroll=LRU_UNROLL)

    carry_in = []
    for c in range(n_slabs):
        cur = hcar[c, 0:1, :]
        rows = [cur]
        for j in range(1, nch):
            cur = h_end[c][j - 1:j, :] + decay_end[c][j - 1:j, :] * cur
            rows.append(cur)
        carry_in.append(jnp.concatenate(rows, axis=0))
        hcar[c] = jnp.broadcast_to(h_end[c][nch - 1:nch, :] + decay_end[c][nch - 1:nch, :] * cur, (nch, LANES))

    for c in range(n_slabs):
        def out_step(i, carry, c=c):
            h = b_s[c, chain_rows(i), :] + a_s[c, chain_rows(i), :] * carry_in[c]
            gated = pl.ds(SUBLANES + i, nch, stride=pitch)
            outs[c, gated, :] = h * ys[c, gated, :]
            return carry

        lax.fori_loop(0, steps, out_step, 0, unroll=LRU_UNROLL)
    for c in range(n_slabs):
        for j in range(nch):
            o_ref[0, j * steps:(j + 1) * steps, lanes[c]] = (
                outs[c, j * pitch + SUBLANES:(j + 1) * pitch, :].astype(BF16))


def _lru(z, cw, cb, wrg, brg, wig, big, lam, *, tc):
    b, s, _ = z.shape
    w = cw.shape[1]
    n_slabs = w // LANES
    assert n_slabs == wrg.shape[0] and tc % (SUBLANES * SUBLANES) == 0
    chain_rows = tc + SUBLANES * SUBLANES
    slab = lambda rows: pltpu.VMEM((n_slabs, rows, LANES), F32)
    return pl.pallas_call(
        functools.partial(_lru_kernel, tc=tc),
        grid=(b, s // tc),
        in_specs=[
            pl.BlockSpec((1, tc, w), lambda bi, i: (bi, i, 0)),
            pl.BlockSpec((1, tc, w), lambda bi, i: (bi, i, 1)),
            _resident(cw.shape), _resident(cb.shape), _resident(wrg.shape), _resident(brg.shape),
            _resident(wig.shape), _resident(big.shape), _resident(lam.shape),
        ],
        out_specs=pl.BlockSpec((1, tc, w), lambda bi, i: (bi, i, 0)),
        out_shape=jax.ShapeDtypeStruct((b, s, w), BF16),
        scratch_shapes=[slab(chain_rows), slab(chain_rows), slab(chain_rows), slab(tc), slab(tc),
                        slab(SUBLANES), slab(SUBLANES)],
        compiler_params=pltpu.CompilerParams(
            dimension_semantics=("arbitrary", "arbitrary"), vmem_limit_bytes=VMEM_LIMIT),
        name="lru",
    )(z, z, cw, cb, wrg, brg, wig, big, lam)


def _merge_kernel(o_ref, lg_ref, ga_ref, gb_ref, h_ref, womla_ref, wolru_ref, wout_ref, g_ref, out_ref):
    o_mla = _dot(o_ref[0], womla_ref[...])
    o_lru = _dot(lg_ref[0], wolru_ref[...])
    merged = jax.nn.sigmoid(ga_ref[0]) * o_mla + jax.nn.sigmoid(gb_ref[0]) * o_lru
    y = _dot(merged.astype(BF16), wout_ref[...])
    out_ref[0] = h_ref[0] + _rms(y, g_ref[...])


def _merge(o, lg, z, h, womla, wolru, wout, g, *, tm):
    b, s, d = h.shape
    tok = lambda col: pl.BlockSpec((1, tm, d), lambda bi, i: (bi, i, col))
    return pl.pallas_call(
        _merge_kernel,
        grid=(b, s // tm),
        in_specs=[
            pl.BlockSpec((1, tm, o.shape[2]), lambda bi, i: (bi, i, 0)),
            tok(0), tok(2), tok(3), tok(0),
            _resident(womla.shape), _resident(wolru.shape), _resident(wout.shape), _resident(g.shape),
        ],
        out_specs=tok(0),
        out_shape=jax.ShapeDtypeStruct((b, s, d), F32),
        compiler_params=pltpu.CompilerParams(
            dimension_semantics=("arbitrary", "arbitrary"), vmem_limit_bytes=VMEM_LIMIT),
        name="merge",
    )(o, lg, z, z, h, womla, wolru, wout, g)


def _pad_heads(w, per_head):
    k = w.shape[0]
    w = w.reshape(k, MLA_HEADS, per_head)
    return jnp.pad(w, ((0, 0), (0, 0), (0, HEAD_PAD - per_head))).reshape(k, MLA_HEADS * HEAD_PAD)


def _rope_consts():
    half = QK_ROPE // 2
    inv_freq = ROPE_THETA ** (-jnp.arange(half, dtype=F32) / half)
    lane = jnp.arange(HEAD_PAD)
    lo = (lane >= QK_NOPE) & (lane < QK_NOPE + half)
    hi = (lane >= QK_NOPE + half) & (lane < QK_NOPE + QK_ROPE)
    freq = jnp.where(lo | hi, inv_freq[(lane - QK_NOPE) % half], 0.0)
    rows = jnp.stack([freq, jnp.where(lo, -1.0, 0.0), jnp.where(hi, 1.0, 0.0),
                      jnp.where(lane == SHIFT_LANE, 1.0, 0.0)]).astype(F32)
    return jnp.pad(rows, ((0, SUBLANES - rows.shape[0]), (0, 0)))


def kernel(x, positions, ffn1_pre_g, ffn1_w_gate, ffn1_w_up, ffn1_w_down, ffn1_post_g, mix_pre_g, w_in, q_norm_g, w_uq, kv_norm_g, w_ukv, w_o_mla, conv_w, conv_b, w_rg, b_rg, w_ig, b_ig, lru_lambda, w_o_lru, w_out, mix_post_g, ffn2_pre_g, ffn2_w_gate, ffn2_w_up, ffn2_w_down, ffn2_post_g):
    b, s, d = x.shape
    depth = w_in.shape[0]
    tm = min(512, s)
    tq = 2 * ATTN_TK
    tc = min(512, s)
    ff_chunk = ffn1_w_gate.shape[2] // 2
    q_scale = (QK_NOPE + QK_ROPE) ** -0.5 * LOG2E
    n_lat = Q_LORA + KV_LORA
    rope_consts = _rope_consts()
    vones = jnp.tile((jnp.arange(V_ROWS) == V_HEAD).astype(F32), MLA_HEADS).reshape(-1, 1)
    pos_col = positions.reshape(b, s, 1)
    pos_row = positions.reshape(b, 1, s)
    half = QK_ROPE // 2
    freq_col = (ROPE_THETA ** (-jnp.arange(half, dtype=F32) / half)).reshape(half, 1)
    row = lambda v: v.reshape(1, -1)

    h = x.reshape(b * s, d)
    for l in range(depth):
        h = _ffn(h, row(ffn1_pre_g[l]), ffn1_w_gate[l].astype(BF16), ffn1_w_up[l].astype(BF16),
                 ffn1_w_down[l].astype(BF16), row(ffn1_post_g[l]), tm=tm, ff_chunk=ff_chunk)

        wi = w_in[l]
        w_krope = jnp.pad(wi[:, n_lat:n_lat + QK_ROPE], ((0, 0), (QK_NOPE, HEAD_PAD - QK_NOPE - QK_ROPE)))
        wlat = jnp.concatenate([wi[:, :n_lat], w_krope], axis=1).astype(BF16)
        wrest = wi[:, n_lat + QK_ROPE:].astype(BF16)
        wuqt = _pad_heads(w_uq[l], QK_NOPE + QK_ROPE).T.astype(BF16)
        wkv = w_ukv[l].reshape(KV_LORA, MLA_HEADS, QK_NOPE + V_HEAD)
        wk = _pad_heads(wkv[:, :, :QK_NOPE].reshape(KV_LORA, -1), QK_NOPE).astype(BF16)
        wvt = jnp.pad(wkv[:, :, QK_NOPE:], ((0, 0), (0, 0), (0, V_ROWS - V_HEAD)))
        wvt = wvt.reshape(KV_LORA, MLA_HEADS * V_ROWS).T.astype(BF16)
        h3 = h.reshape(b, s, d)
        qt, k, vt, z = _inproj(h3, pos_col, pos_row, row(mix_pre_g[l]), wlat, wrest, row(q_norm_g[l]), wuqt,
                               row(kv_norm_g[l]), wk, wvt, vones, rope_consts, freq_col,
                               tm=tm, q_scale=q_scale)

        o = _attn(qt, k, vt, tq=tq)
        lg = _lru(z, conv_w[l], row(conv_b[l]), (0.5 * w_rg[l]).astype(BF16), row(0.5 * b_rg[l]),
                  (0.5 * w_ig[l]).astype(BF16), row(0.5 * b_ig[l]), row(lru_lambda[l]), tc=tc)

        h = _merge(o, lg, z, h3, w_o_mla[l].astype(BF16), w_o_lru[l].astype(BF16), w_out[l].astype(BF16),
                   row(mix_post_g[l]), tm=tm).reshape(b * s, d)

        h = _ffn(h, row(ffn2_pre_g[l]), ffn2_w_gate[l].astype(BF16), ffn2_w_up[l].astype(BF16),
                 ffn2_w_down[l].astype(BF16), row(ffn2_post_g[l]), tm=tm, ff_chunk=ff_chunk)
    return h.reshape(b, s, d)
```

```python
import functools
import math

import jax
import jax.numpy as jnp
from jax import lax
from jax.experimental import pallas as pl
from jax.experimental.pallas import tpu as pltpu

MLA_HEADS = 8
QK_NOPE = 64
QK_ROPE = 32
V_HEAD = 64
Q_LORA = 384
KV_LORA = 256
ROPE_THETA = 10000.0
LRU_BLOCKS = 8
CONV_WIDTH = 4
LRU_C = 8.0
FFN_RES_WEIGHT = 0.5
NORM_EPS = 1e-6

LANES = 128
SUBLANES = 8
BF16_ROWS = 16
HEAD_PAD = LANES
V_ROWS = V_HEAD + BF16_ROWS
SHIFT_LANE = QK_NOPE + QK_ROPE
ATTN_TK = 256
ATTN_HEADS = 4
ATTN_PAIRS = 4
LRU_UNROLL = 16
INPROJ_Z_CHUNKS = 8
VMEM_LIMIT = 56 * 1024 * 1024
LOG2E = math.log2(math.e)

F32 = jnp.float32
BF16 = jnp.bfloat16
NT_DIMS = (((1,), (1,)), ((), ()))


def _rms(x, g):
    return x * lax.rsqrt(jnp.mean(x * x, axis=-1, keepdims=True) + NORM_EPS) * g


def _dot(a, b):
    return jnp.dot(a, b, preferred_element_type=F32)


def _resident(shape):
    nd = len(shape)
    return pl.BlockSpec(shape, lambda *_: (0,) * nd, pipeline_mode=pl.Buffered(1))


def _ffn_kernel(h_ref, pre_g_ref, wg_ref, wu_ref, wd_ref, post_g_ref, o_ref, act_ref, *, ff_chunk):
    x = h_ref[...]
    xn = _rms(x, pre_g_ref[...]).astype(BF16)
    d_ff = wg_ref.shape[1]
    for c in range(d_ff // ff_chunk):
        sl = slice(c * ff_chunk, (c + 1) * ff_chunk)
        g = _dot(xn, wg_ref[:, sl])
        u = _dot(xn, wu_ref[:, sl])
        act_ref[:, sl] = (g * jax.nn.sigmoid(g) * u).astype(BF16)
    y = _dot(act_ref[...], wd_ref[...])
    o_ref[...] = x + FFN_RES_WEIGHT * _rms(y, post_g_ref[...])


def _ffn(h, pre_g, wg, wu, wd, post_g, *, tm, ff_chunk):
    t, d = h.shape
    d_ff = wg.shape[1]
    return pl.pallas_call(
        functools.partial(_ffn_kernel, ff_chunk=ff_chunk),
        grid=(t // tm,),
        in_specs=[
            pl.BlockSpec((tm, d), lambda i: (i, 0)),
            _resident((1, d)),
            _resident((d, d_ff)),
            _resident((d, d_ff)),
            _resident((d_ff, d)),
            _resident((1, d)),
        ],
        out_specs=pl.BlockSpec((tm, d), lambda i: (i, 0)),
        out_shape=jax.ShapeDtypeStruct((t, d), F32),
        scratch_shapes=[pltpu.VMEM((tm, d_ff), BF16)],
        compiler_params=pltpu.CompilerParams(
            dimension_semantics=("arbitrary",), vmem_limit_bytes=VMEM_LIMIT),
        name="ffn",
    )(h, pre_g, wg, wu, wd, post_g)


def _inproj_kernel(h_ref, pos_col_ref, pos_row_ref, g_ref, wlat_ref, wrest_ref, qg_ref, wuqt_ref, kvg_ref,
                   wk_ref, wvt_ref, vones_ref, rope_ref, freq_col_ref, qt_ref, k_ref, vt_ref, z_ref, *,
                   q_scale):
    u = _rms(h_ref[0], g_ref[...]).astype(BF16)
    zc = wrest_ref.shape[1] // INPROJ_Z_CHUNKS

    def z_chunk(c):
        z_ref[0, :, c * zc:(c + 1) * zc] = _dot(u, wrest_ref[:, c * zc:(c + 1) * zc])

    half = QK_ROPE // 2
    lat = _dot(u, wlat_ref[...])
    z_chunk(0)
    ang_t = freq_col_ref[...] * pos_row_ref[0].astype(F32)
    cos_t = jnp.cos(ang_t)
    sin_t = jnp.sin(ang_t)
    ang = pos_col_ref[0].astype(F32) * rope_ref[0:1, :]
    cos = jnp.cos(ang)
    sin = jnp.sin(ang)
    sin_lo = sin * rope_ref[1:2, :]
    sin_hi = sin * rope_ref[2:3, :]
    z_chunk(1)
    qn = _rms(lat[:, :Q_LORA], qg_ref[...]).astype(BF16)
    kvn = _rms(lat[:, Q_LORA:Q_LORA + KV_LORA], kvg_ref[...]).astype(BF16)
    k_rope = lat[:, Q_LORA + KV_LORA:]
    qt = lax.dot_general(wuqt_ref[...], qn, NT_DIMS, preferred_element_type=F32)
    z_chunk(2)
    k = _dot(kvn, wk_ref[...])
    vt = lax.dot_general(wvt_ref[...], kvn, NT_DIMS, preferred_element_type=F32) + vones_ref[...]
    z_chunk(3)

    n_heads = qt.shape[0] // HEAD_PAD
    for h in range(n_heads):
        base = h * HEAD_PAD
        x1 = qt[base + QK_NOPE:base + QK_NOPE + half]
        x2 = qt[base + QK_NOPE + half:base + QK_NOPE + QK_ROPE]
        tile = jnp.concatenate([qt[base:base + QK_NOPE], x1 * cos_t - x2 * sin_t, x2 * cos_t + x1 * sin_t,
                                qt[base + QK_NOPE + QK_ROPE:base + HEAD_PAD]], axis=0)
        qt_ref[0, base:base + HEAD_PAD, :] = (tile * q_scale).astype(BF16)
        if h == n_heads // 2 - 1:
            z_chunk(4)
    z_chunk(5)

    k_rope = (k_rope * cos + pltpu.roll(k_rope, HEAD_PAD - half, 1) * sin_lo
              + pltpu.roll(k_rope, half, 1) * sin_hi) + rope_ref[3:4, :]
    for h in range(n_heads):
        sl = slice(h * HEAD_PAD, (h + 1) * HEAD_PAD)
        k_ref[0, :, sl] = (k[:, sl] + k_rope).astype(BF16)
    z_chunk(6)
    for c in range(vt_ref.shape[1]):
        vt_ref[0, c] = vt[:, c * ATTN_TK:(c + 1) * ATTN_TK].astype(BF16)
    z_chunk(7)


def _inproj(h, pos_col, pos_row, g, wlat, wrest, qg, wuqt, kvg, wk, wvt, vones, rope_consts, freq_col, *,
            tm, q_scale):
    b, s, d = h.shape
    hp = wuqt.shape[0]
    n_rest = wrest.shape[1]
    vrows = wvt.shape[0]
    tok = lambda w: pl.BlockSpec((1, tm, w), lambda bi, i: (bi, i, 0))
    consts = (g, wlat, wrest, qg, wuqt, kvg, wk, wvt, vones, rope_consts, freq_col)
    return pl.pallas_call(
        functools.partial(_inproj_kernel, q_scale=q_scale),
        grid=(b, s // tm),
        in_specs=[tok(d), tok(1), pl.BlockSpec((1, 1, tm), lambda bi, i: (bi, 0, i))]
                 + [_resident(c.shape) for c in consts],
        out_specs=[pl.BlockSpec((1, hp, tm), lambda bi, i: (bi, 0, i)), tok(hp),
                   pl.BlockSpec((1, tm // ATTN_TK, vrows, ATTN_TK), lambda bi, i: (bi, i, 0, 0)),
                   tok(n_rest)],
        out_shape=[
            jax.ShapeDtypeStruct((b, hp, s), BF16),
            jax.ShapeDtypeStruct((b, s, hp), BF16),
            jax.ShapeDtypeStruct((b, s // ATTN_TK, vrows, ATTN_TK), BF16),
            jax.ShapeDtypeStruct((b, s, n_rest), F32),
        ],
        compiler_params=pltpu.CompilerParams(
            dimension_semantics=("arbitrary", "arbitrary"), vmem_limit_bytes=VMEM_LIMIT),
        name="inproj",
    )(h, pos_col, pos_row, *consts)


def _attn_kernel(qt_ref, k_ref, vt_ref, o_ref, *scratch, tq):
    tk = ATTN_TK
    nh = ATTN_HEADS
    qi = pl.program_id(2)
    per_tile = tq // tk
    half = tq // 2
    first = per_tile * qi
    acc_refs, p_even, p_odd = scratch[:nh], scratch[nh:2 * nh], scratch[2 * nh:]
    causal = (lax.broadcasted_iota(jnp.int32, (tk, tq), 0) <= lax.broadcasted_iota(jnp.int32, (tk, tq), 1))

    def keys(h, blk):
        return k_ref[0, pl.ds(pl.multiple_of(blk * tk, tk), tk), h * HEAD_PAD:(h + 1) * HEAD_PAD]

    def values_t(h, blk):
        return vt_ref[0, blk, h * V_ROWS:(h + 1) * V_ROWS, :]

    def finish(acc):
        return acc[:V_HEAD] / acc[V_HEAD:V_HEAD + 1]

    def write(outs):
        o_ref[0] = jnp.concatenate(outs, axis=0).T.astype(BF16)

    qts = []
    for h in range(nh):
        qt = qt_ref[0, h * HEAD_PAD:(h + 1) * HEAD_PAD, :]
        s0 = _dot(keys(h, 0), qt)
        shift = jnp.max(jnp.where(jnp.logical_or(qi > 0, causal), s0, -jnp.inf), axis=0, keepdims=True)
        p_even[h][...] = jnp.exp2(s0 - shift).astype(BF16)
        rows = lax.broadcasted_iota(jnp.int32, (BF16_ROWS, tq), 0)
        shift_rows = jnp.where(rows == 0, -shift, 0.0).astype(BF16)
        qts.append(jnp.concatenate([qt[:SHIFT_LANE], shift_rows, qt[SHIFT_LANE + BF16_ROWS:]], axis=0))
        acc_refs[h][...] = jnp.zeros((V_ROWS, tq), F32)

    def prefetch(h, blk, p_ref):
        p_ref[...] = jnp.exp2(_dot(keys(h, blk), qts[h])).astype(BF16)

    def consume(h, blk, p_ref):
        acc_refs[h][...] += _dot(values_t(h, blk), p_ref[...])

    def pair(u, carry):
        blk = per_tile * u
        for h in range(nh):
            prefetch(h, blk + 1, p_odd[h])
        for h in range(nh):
            consume(h, blk, p_even[h])
        for h in range(nh):
            prefetch(h, blk + 2, p_even[h])
        for h in range(nh):
            consume(h, blk + 1, p_odd[h])
        return carry

    def pair_group(ug, carry):
        for j in range(ATTN_PAIRS):
            carry = pair(ATTN_PAIRS * ug + j, carry)
        return carry

    n_groups = qi >> (ATTN_PAIRS.bit_length() - 1)
    lax.fori_loop(0, n_groups, pair_group, 0)
    lax.fori_loop(n_groups * ATTN_PAIRS, qi, pair, 0)

    accs = []
    for h in range(nh):
        p0 = jnp.where(causal, p_even[h][...], 0.0)
        s1 = jnp.where(causal[:, :half], _dot(keys(h, first + 1), qts[h][:, half:]), -jnp.inf)
        acc = acc_refs[h][...] + _dot(values_t(h, first), p0)
        pv1 = _dot(values_t(h, first + 1), jnp.exp2(s1).astype(BF16))
        accs.append(jnp.concatenate([acc[:, :half], acc[:, half:] + pv1], axis=1))
    write([finish(acc) for acc in accs])
    overflow = jnp.max(jnp.where(jnp.isfinite(jnp.concatenate(accs, axis=0)), 0.0, 1.0)) > 0.0

    @pl.when(overflow)
    def _():
        outs = []
        for h in range(nh):
            qt = qt_ref[0, h * HEAD_PAD:(h + 1) * HEAD_PAD, :]

            def update(s, blk, state, h=h):
                m, acc = state
                m_new = jnp.maximum(m, jnp.max(s, axis=0, keepdims=True))
                p = jnp.exp2(s - m_new).astype(BF16)
                return m_new, jnp.exp2(m - m_new) * acc + _dot(values_t(h, blk), p)

            def full_block(blk, state, h=h, qt=qt, update=update):
                return update(_dot(keys(h, blk), qt), blk, state)

            state = (jnp.full((1, tq), -jnp.inf, F32), jnp.zeros((V_ROWS, tq), F32))
            state = lax.fori_loop(0, first, full_block, state)
            for d in range(per_tile):
                s = jnp.where(lax.broadcasted_iota(jnp.int32, (tk, tq), 0) + d * tk
                              <= lax.broadcasted_iota(jnp.int32, (tk, tq), 1),
                              _dot(keys(h, first + d), qt), -jnp.inf)
                state = update(s, first + d, state)
            outs.append(finish(state[1]))
        write(outs)


def _attn(qt, k, vt, *, tq):
    b, s, hp = k.shape
    nh = ATTN_HEADS
    groups = hp // (nh * HEAD_PAD)
    assert tq == 2 * ATTN_TK, "the key loop consumes blocks in pairs"
    return pl.pallas_call(
        functools.partial(_attn_kernel, tq=tq),
        grid=(b, groups, s // tq),
        in_specs=[
            pl.BlockSpec((1, nh * HEAD_PAD, tq), lambda bi, g, i: (bi, g, i)),
            pl.BlockSpec((1, s, nh * HEAD_PAD), lambda bi, g, i: (bi, 0, g)),
            pl.BlockSpec((1, s // ATTN_TK, nh * V_ROWS, ATTN_TK), lambda bi, g, i: (bi, 0, g, 0),
                         pipeline_mode=pl.Buffered(1)),
        ],
        out_specs=pl.BlockSpec((1, tq, nh * V_HEAD), lambda bi, g, i: (bi, i, g)),
        out_shape=jax.ShapeDtypeStruct((b, s, groups * nh * V_HEAD), BF16),
        scratch_shapes=([pltpu.VMEM((V_ROWS, tq), F32)] * nh + [pltpu.VMEM((ATTN_TK, tq), BF16)] * (2 * nh)),
        compiler_params=pltpu.CompilerParams(
            dimension_semantics=("arbitrary", "arbitrary", "arbitrary"), vmem_limit_bytes=VMEM_LIMIT),
        name="attn",
    )(qt, k, vt)


def _lru_kernel(xb_ref, yb_ref, cw_ref, cb_ref, wrg_ref, brg_ref, wig_ref, big_ref, lam_ref, o_ref,
                xs, ys, outs, a_s, b_s, xhalo, hcar, *, tc):
    t = pl.program_id(1)
    nch = SUBLANES
    steps = tc // nch
    pitch = steps + SUBLANES
    n_slabs = xs.shape[0]
    lanes = [slice(c * LANES, (c + 1) * LANES) for c in range(n_slabs)]

    @pl.when(t == 0)
    def _():
        xhalo[...] = jnp.zeros(xhalo.shape, F32)
        hcar[...] = jnp.zeros(hcar.shape, F32)

    for c in range(n_slabs):
        for j in range(nch):
            data = slice(j * pitch + SUBLANES, (j + 1) * pitch)
            xs[c, data, :] = xb_ref[0, j * steps:(j + 1) * steps, lanes[c]]
            ys[c, data, :] = jax.nn.gelu(yb_ref[0, j * steps:(j + 1) * steps, lanes[c]])
            if j == 0:
                xs[c, 0:SUBLANES, :] = xhalo[c]
            else:
                xs[c, j * pitch:j * pitch + SUBLANES, :] = xb_ref[0, j * steps - SUBLANES:j * steps, lanes[c]]
        xhalo[c] = xb_ref[0, tc - SUBLANES:tc, lanes[c]]

    def chain_rows(i):
        return pl.ds(pl.multiple_of(i * nch, nch), nch)

    for c in range(n_slabs):
        taps = [jnp.broadcast_to(cw_ref[kk:kk + 1, lanes[c]], (nch, LANES)) for kk in range(CONV_WIDTH)]
        bias = jnp.broadcast_to(cb_ref[:, lanes[c]], (nch, LANES))

        def conv_step(i, carry, c=c, taps=taps, bias=bias):
            xc = bias
            for kk in range(CONV_WIDTH):
                first = SUBLANES - (CONV_WIDTH - 1) + kk + i
                xc = xc + xs[c, pl.ds(first, nch, stride=pitch), :] * taps[kk]
            b_s[c, chain_rows(i), :] = xc
            return carry

        lax.fori_loop(0, steps, conv_step, 0, unroll=LRU_UNROLL)

    half_rate = (-0.5 * LRU_C * LOG2E) * jax.nn.softplus(-lam_ref[...])
    for c in range(n_slabs):
        xg = b_s[c]
        xg16 = xg.astype(BF16)
        t_r = jnp.tanh(_dot(xg16, wrg_ref[c]) + brg_ref[:, lanes[c]])
        t_i = jnp.tanh(_dot(xg16, wig_ref[c]) + big_ref[:, lanes[c]])
        a = jnp.exp2(half_rate[:, lanes[c]] * t_r + half_rate[:, lanes[c]])
        a_s[c] = a
        b_s[c] = jnp.sqrt(1.0 - a * a) * ((0.5 * t_i + 0.5) * xg)

    def scan_step(i, carry):
        hs, decays = carry
        new_h, new_decay = [], []
        for c in range(n_slabs):
            a = a_s[c, chain_rows(i), :]
            h = a * hs[c] + b_s[c, chain_rows(i), :]
            decay = a * decays[c]
            b_s[c, chain_rows(i), :] = h
            a_s[c, chain_rows(i), :] = decay
            new_h.append(h)
            new_decay.append(decay)
        return tuple(new_h), tuple(new_decay)

    zeros = tuple(jnp.zeros((nch, LANES), F32) for _ in range(n_slabs))
    ones = tuple(jnp.ones((nch, LANES), F32) for _ in range(n_slabs))
    h_end, decay_end = lax.fori_loop(0, steps, scan_step, (zeros, ones), unroll=LRU_UNROLL)

    carry_in = []
    for c in range(n_slabs):
        cur = hcar[c, 0:1, :]
        rows = [cur]
        for j in range(1, nch):
            cur = h_end[c][j - 1:j, :] + decay_end[c][j - 1:j, :] * cur
            rows.append(cur)
        carry_in.append(jnp.concatenate(rows, axis=0))
        hcar[c] = jnp.broadcast_to(h_end[c][nch - 1:nch, :] + decay_end[c][nch - 1:nch, :] * cur, (nch, LANES))

    for c in range(n_slabs):
        def out_step(i, carry, c=c):
            h = b_s[c, chain_rows(i), :] + a_s[c, chain_rows(i), :] * carry_in[c]
            gated = pl.ds(SUBLANES + i, nch, stride=pitch)
            outs[c, gated, :] = h * ys[c, gated, :]
            return carry

        lax.fori_loop(0, steps, out_step, 0, unroll=LRU_UNROLL)
    for c in range(n_slabs):
        for j in range(nch):
            o_ref[0, j * steps:(j + 1) * steps, lanes[c]] = (
                outs[c, j * pitch + SUBLANES:(j + 1) * pitch, :].astype(BF16))


def _lru(z, cw, cb, wrg, brg, wig, big, lam, *, tc):
    b, s, _ = z.shape
    w = cw.shape[1]
    n_slabs = w // LANES
    assert n_slabs == wrg.shape[0] and tc % (SUBLANES * SUBLANES) == 0
    chain_rows = tc + SUBLANES * SUBLANES
    slab = lambda rows: pltpu.VMEM((n_slabs, rows, LANES), F32)
    return pl.pallas_call(
        functools.partial(_lru_kernel, tc=tc),
        grid=(b, s // tc),
        in_specs=[
            pl.BlockSpec((1, tc, w), lambda bi, i: (bi, i, 0)),
            pl.BlockSpec((1, tc, w), lambda bi, i: (bi, i, 1)),
            _resident(cw.shape), _resident(cb.shape), _resident(wrg.shape), _resident(brg.shape),
            _resident(wig.shape), _resident(big.shape), _resident(lam.shape),
        ],
        out_specs=pl.BlockSpec((1, tc, w), lambda bi, i: (bi, i, 0)),
        out_shape=jax.ShapeDtypeStruct((b, s, w), BF16),
        scratch_shapes=[slab(chain_rows), slab(chain_rows), slab(chain_rows), slab(tc), slab(tc),
                        slab(SUBLANES), slab(SUBLANES)],
        compiler_params=pltpu.CompilerParams(
            dimension_semantics=("arbitrary", "arbitrary"), vmem_limit_bytes=VMEM_LIMIT),
        name="lru",
    )(z, z, cw, cb, wrg, brg, wig, big, lam)


def _merge_kernel(o_ref, lg_ref, ga_ref, gb_ref, h_ref, womla_ref, wolru_ref, wout_ref, g_ref, out_ref):
    o_mla = _dot(o_ref[0], womla_ref[...])
    o_lru = _dot(lg_ref[0], wolru_ref[...])
    merged = jax.nn.sigmoid(ga_ref[0]) * o_mla + jax.nn.sigmoid(gb_ref[0]) * o_lru
    y = _dot(merged.astype(BF16), wout_ref[...])
    out_ref[0] = h_ref[0] + _rms(y, g_ref[...])


def _merge(o, lg, z, h, womla, wolru, wout, g, *, tm):
    b, s, d = h.shape
    tok = lambda col: pl.BlockSpec((1, tm, d), lambda bi, i: (bi, i, col))
    return pl.pallas_call(
        _merge_kernel,
        grid=(b, s // tm),
        in_specs=[
            pl.BlockSpec((1, tm, o.shape[2]), lambda bi, i: (bi, i, 0)),
            tok(0), tok(2), tok(3), tok(0),
            _resident(womla.shape), _resident(wolru.shape), _resident(wout.shape), _resident(g.shape),
        ],
        out_specs=tok(0),
        out_shape=jax.ShapeDtypeStruct((b, s, d), F32),
        compiler_params=pltpu.CompilerParams(
            dimension_semantics=("arbitrary", "arbitrary"), vmem_limit_bytes=VMEM_LIMIT),
        name="merge",
    )(o, lg, z, z, h, womla, wolru, wout, g)


def _pad_heads(w, per_head):
    k = w.shape[0]
    w = w.reshape(k, MLA_HEADS, per_head)
    return jnp.pad(w, ((0, 0), (0, 0), (0, HEAD_PAD - per_head))).reshape(k, MLA_HEADS * HEAD_PAD)


def _rope_consts():
    half = QK_ROPE // 2
    inv_freq = ROPE_THETA ** (-jnp.arange(half, dtype=F32) / half)
    lane = jnp.arange(HEAD_PAD)
    lo = (lane >= QK_NOPE) & (lane < QK_NOPE + half)
    hi = (lane >= QK_NOPE + half) & (lane < QK_NOPE + QK_ROPE)
    freq = jnp.where(lo | hi, inv_freq[(lane - QK_NOPE) % half], 0.0)
    rows = jnp.stack([freq, jnp.where(lo, -1.0, 0.0), jnp.where(hi, 1.0, 0.0),
                      jnp.where(lane == SHIFT_LANE, 1.0, 0.0)]).astype(F32)
    return jnp.pad(rows, ((0, SUBLANES - rows.shape[0]), (0, 0)))


def kernel(x, positions, ffn1_pre_g, ffn1_w_gate, ffn1_w_up, ffn1_w_down, ffn1_post_g, mix_pre_g, w_in, q_norm_g, w_uq, kv_norm_g, w_ukv, w_o_mla, conv_w, conv_b, w_rg, b_rg, w_ig, b_ig, lru_lambda, w_o_lru, w_out, mix_post_g, ffn2_pre_g, ffn2_w_gate, ffn2_w_up, ffn2_w_down, ffn2_post_g):
    b, s, d = x.shape
    depth = w_in.shape[0]
    tm = min(512, s)
    tq = 2 * ATTN_TK
    tc = min(512, s)
    ff_chunk = ffn1_w_gate.shape[2] // 2
    q_scale = (QK_NOPE + QK_ROPE) ** -0.5 * LOG2E
    n_lat = Q_LORA + KV_LORA
    rope_consts = _rope_consts()
    vones = jnp.tile((jnp.arange(V_ROWS) == V_HEAD).astype(F32), MLA_HEADS).reshape(-1, 1)
    pos_col = positions.reshape(b, s, 1)
    pos_row = positions.reshape(b, 1, s)
    half = QK_ROPE // 2
    freq_col = (ROPE_THETA ** (-jnp.arange(half, dtype=F32) / half)).reshape(half, 1)
    row = lambda v: v.reshape(1, -1)

    h = x.reshape(b * s, d)
    for l in range(depth):
        h = _ffn(h, row(ffn1_pre_g[l]), ffn1_w_gate[l].astype(BF16), ffn1_w_up[l].astype(BF16),
                 ffn1_w_down[l].astype(BF16), row(ffn1_post_g[l]), tm=tm, ff_chunk=ff_chunk)

        wi = w_in[l]
        w_krope = jnp.pad(wi[:, n_lat:n_lat + QK_ROPE], ((0, 0), (QK_NOPE, HEAD_PAD - QK_NOPE - QK_ROPE)))
        wlat = jnp.concatenate([wi[:, :n_lat], w_krope], axis=1).astype(BF16)
        wrest = wi[:, n_lat + QK_ROPE:].astype(BF16)
        wuqt = _pad_heads(w_uq[l], QK_NOPE + QK_ROPE).T.astype(BF16)
        wkv = w_ukv[l].reshape(KV_LORA, MLA_HEADS, QK_NOPE + V_HEAD)
        wk = _pad_heads(wkv[:, :, :QK_NOPE].reshape(KV_LORA, -1), QK_NOPE).astype(BF16)
        wvt = jnp.pad(wkv[:, :, QK_NOPE:], ((0, 0), (0, 0), (0, V_ROWS - V_HEAD)))
        wvt = wvt.reshape(KV_LORA, MLA_HEADS * V_ROWS).T.astype(BF16)
        h3 = h.reshape(b, s, d)
        qt, k, vt, z = _inproj(h3, pos_col, pos_row, row(mix_pre_g[l]), wlat, wrest, row(q_norm_g[l]), wuqt,
                               row(kv_norm_g[l]), wk, wvt, vones, rope_consts, freq_col,
                               tm=tm, q_scale=q_scale)

        o = _attn(qt, k, vt, tq=tq)
        lg = _lru(z, conv_w[l], row(conv_b[l]), (0.5 * w_rg[l]).astype(BF16), row(0.5 * b_rg[l]),
                  (0.5 * w_ig[l]).astype(BF16), row(0.5 * b_ig[l]), row(lru_lambda[l]), tc=tc)

        h = _merge(o, lg, z, h3, w_o_mla[l].astype(BF16), w_o_lru[l].astype(BF16), w_out[l].astype(BF16),
                   row(mix_post_g[l]), tm=tm).reshape(b * s, d)

        h = _ffn(h, row(ffn2_pre_g[l]), ffn2_w_gate[l].astype(BF16), ffn2_w_up[l].astype(BF16),
                 ffn2_w_down[l].astype(BF16), row(ffn2_post_g[l]), tm=tm, ff_chunk=ff_chunk)
    return h.reshape(b, s, d)
```

```python
import functools
import math

import jax
import jax.numpy as jnp
from jax import lax
from jax.experimental import pallas as pl
from jax.experimental.pallas import tpu as pltpu

MLA_HEADS = 8
QK_NOPE = 64
QK_ROPE = 32
V_HEAD = 64
Q_LORA = 384
KV_LORA = 256
ROPE_THETA = 10000.0
LRU_BLOCKS = 8
CONV_WIDTH = 4
LRU_C = 8.0
FFN_RES_WEIGHT = 0.5
NORM_EPS = 1e-6

LANES = 128
SUBLANES = 8
BF16_ROWS = 16
HEAD_PAD = LANES
V_ROWS = V_HEAD + BF16_ROWS
SHIFT_LANE = QK_NOPE + QK_ROPE
ATTN_TK = 256
ATTN_HEADS = 4
ATTN_PAIRS = 4
LRU_UNROLL = 16
INPROJ_Z_CHUNKS = 8
VMEM_LIMIT = 56 * 1024 * 1024
LOG2E = math.log2(math.e)

F32 = jnp.float32
BF16 = jnp.bfloat16
NT_DIMS = (((1,), (1,)), ((), ()))


def _rms(x, g):
    return x * lax.rsqrt(jnp.mean(x * x, axis=-1, keepdims=True) + NORM_EPS) * g


def _dot(a, b):
    return jnp.dot(a, b, preferred_element_type=F32)


def _resident(shape):
    nd = len(shape)
    return pl.BlockSpec(shape, lambda *_: (0,) * nd, pipeline_mode=pl.Buffered(1))


def _ffn_kernel(h_ref, pre_g_ref, wg_ref, wu_ref, wd_ref, post_g_ref, o_ref, act_ref, *, ff_chunk):
    n_chunks = wg_ref.shape[1] // ff_chunk
    rows = h_ref.shape[0] // 2
    first, second = slice(0, rows), slice(rows, 2 * rows)

    def up(xn, rs, c):
        sl = slice(c * ff_chunk, (c + 1) * ff_chunk)
        g = _dot(xn, wg_ref[:, sl])
        u = _dot(xn, wu_ref[:, sl])
        act_ref[rs, sl] = (g * jax.nn.sigmoid(g) * u).astype(BF16)

    def down(rs):
        y = _dot(act_ref[rs, :], wd_ref[...])
        o_ref[rs, :] = h_ref[rs, :] + FFN_RES_WEIGHT * _rms(y, post_g_ref[...])

    xn_first = _rms(h_ref[first, :], pre_g_ref[...]).astype(BF16)
    up(xn_first, first, 0)
    xn_second = _rms(h_ref[second, :], pre_g_ref[...]).astype(BF16)
    for c in range(1, n_chunks):
        up(xn_first, first, c)
    up(xn_second, second, 0)
    down(first)
    for c in range(1, n_chunks):
        up(xn_second, second, c)
    down(second)


def _ffn(h, pre_g, wg, wu, wd, post_g, *, tm, ff_chunk):
    t, d = h.shape
    d_ff = wg.shape[1]
    return pl.pallas_call(
        functools.partial(_ffn_kernel, ff_chunk=ff_chunk),
        grid=(t // tm,),
        in_specs=[
            pl.BlockSpec((tm, d), lambda i: (i, 0)),
            _resident((1, d)),
            _resident((d, d_ff)),
            _resident((d, d_ff)),
            _resident((d_ff, d)),
            _resident((1, d)),
        ],
        out_specs=pl.BlockSpec((tm, d), lambda i: (i, 0)),
        out_shape=jax.ShapeDtypeStruct((t, d), F32),
        scratch_shapes=[pltpu.VMEM((tm, d_ff), BF16)],
        compiler_params=pltpu.CompilerParams(
            dimension_semantics=("arbitrary",), vmem_limit_bytes=VMEM_LIMIT),
        name="ffn",
    )(h, pre_g, wg, wu, wd, post_g)


def _inproj_kernel(h_ref, pos_col_ref, pos_row_ref, g_ref, wlat_ref, wrest_ref, qg_ref, wuqt_ref, kvg_ref,
                   wk_ref, wvt_ref, vones_ref, rope_ref, freq_col_ref, qt_ref, k_ref, vt_ref, z_ref, *,
                   q_scale):
    u = _rms(h_ref[0], g_ref[...]).astype(BF16)
    zc = wrest_ref.shape[1] // INPROJ_Z_CHUNKS

    def z_chunk(c):
        z_ref[0, :, c * zc:(c + 1) * zc] = _dot(u, wrest_ref[:, c * zc:(c + 1) * zc])

    half = QK_ROPE // 2
    lat = _dot(u, wlat_ref[...])
    z_chunk(0)
    ang_t = freq_col_ref[...] * pos_row_ref[0].astype(F32)
    cos_t = jnp.cos(ang_t)
    sin_t = jnp.sin(ang_t)
    ang = pos_col_ref[0].astype(F32) * rope_ref[0:1, :]
    cos = jnp.cos(ang)
    sin = jnp.sin(ang)
    sin_lo = sin * rope_ref[1:2, :]
    sin_hi = sin * rope_ref[2:3, :]
    z_chunk(1)
    qn = _rms(lat[:, :Q_LORA], qg_ref[...]).astype(BF16)
    kvn = _rms(lat[:, Q_LORA:Q_LORA + KV_LORA], kvg_ref[...]).astype(BF16)
    k_rope = lat[:, Q_LORA + KV_LORA:]
    qt = lax.dot_general(wuqt_ref[...], qn, NT_DIMS, preferred_element_type=F32)
    z_chunk(2)
    k = _dot(kvn, wk_ref[...])
    vt = lax.dot_general(wvt_ref[...], kvn, NT_DIMS, preferred_element_type=F32) + vones_ref[...]
    z_chunk(3)

    n_heads = qt.shape[0] // HEAD_PAD
    for h in range(n_heads):
        base = h * HEAD_PAD
        x1 = qt[base + QK_NOPE:base + QK_NOPE + half]
        x2 = qt[base + QK_NOPE + half:base + QK_NOPE + QK_ROPE]
        tile = jnp.concatenate([qt[base:base + QK_NOPE], x1 * cos_t - x2 * sin_t, x2 * cos_t + x1 * sin_t,
                                qt[base + QK_NOPE + QK_ROPE:base + HEAD_PAD]], axis=0)
        qt_ref[0, base:base + HEAD_PAD, :] = (tile * q_scale).astype(BF16)
        if h == n_heads // 2 - 1:
            z_chunk(4)
    z_chunk(5)

    k_rope = (k_rope * cos + pltpu.roll(k_rope, HEAD_PAD - half, 1) * sin_lo
              + pltpu.roll(k_rope, half, 1) * sin_hi) + rope_ref[3:4, :]
    for h in range(n_heads):
        sl = slice(h * HEAD_PAD, (h + 1) * HEAD_PAD)
        k_ref[0, :, sl] = (k[:, sl] + k_rope).astype(BF16)
    z_chunk(6)
    for c in range(vt_ref.shape[1]):
        vt_ref[0, c] = vt[:, c * ATTN_TK:(c + 1) * ATTN_TK].astype(BF16)
    z_chunk(7)


def _inproj(h, pos_col, pos_row, g, wlat, wrest, qg, wuqt, kvg, wk, wvt, vones, rope_consts, freq_col, *,
            tm, q_scale):
    b, s, d = h.shape
    hp = wuqt.shape[0]
    n_rest = wrest.shape[1]
    vrows = wvt.shape[0]
    tok = lambda w: pl.BlockSpec((1, tm, w), lambda bi, i: (bi, i, 0))
    consts = (g, wlat, wrest, qg, wuqt, kvg, wk, wvt, vones, rope_consts, freq_col)
    return pl.pallas_call(
        functools.partial(_inproj_kernel, q_scale=q_scale),
        grid=(b, s // tm),
        in_specs=[tok(d), tok(1), pl.BlockSpec((1, 1, tm), lambda bi, i: (bi, 0, i))]
                 + [_resident(c.shape) for c in consts],
        out_specs=[pl.BlockSpec((1, hp, tm), lambda bi, i: (bi, 0, i)), tok(hp),
                   pl.BlockSpec((1, tm // ATTN_TK, vrows, ATTN_TK), lambda bi, i: (bi, i, 0, 0)),
                   tok(n_rest)],
        out_shape=[
            jax.ShapeDtypeStruct((b, hp, s), BF16),
            jax.ShapeDtypeStruct((b, s, hp), BF16),
            jax.ShapeDtypeStruct((b, s // ATTN_TK, vrows, ATTN_TK), BF16),
            jax.ShapeDtypeStruct((b, s, n_rest), F32),
        ],
        compiler_params=pltpu.CompilerParams(
            dimension_semantics=("arbitrary", "arbitrary"), vmem_limit_bytes=VMEM_LIMIT),
        name="inproj",
    )(h, pos_col, pos_row, *consts)


def _attn_kernel(qt_ref, k_ref, vt_ref, o_ref, *scratch, tq):
    tk = ATTN_TK
    nh = ATTN_HEADS
    qi = pl.program_id(2)
    per_tile = tq // tk
    half = tq // 2
    first = per_tile * qi
    acc_refs, p_even, p_odd = scratch[:nh], scratch[nh:2 * nh], scratch[2 * nh:]
    causal = (lax.broadcasted_iota(jnp.int32, (tk, tq), 0) <= lax.broadcasted_iota(jnp.int32, (tk, tq), 1))

    def keys(h, blk):
        return k_ref[0, pl.ds(pl.multiple_of(blk * tk, tk), tk), h * HEAD_PAD:(h + 1) * HEAD_PAD]

    def values_t(h, blk):
        return vt_ref[0, blk, h * V_ROWS:(h + 1) * V_ROWS, :]

    def finish(acc):
        return acc[:V_HEAD] / acc[V_HEAD:V_HEAD + 1]

    def write(outs):
        o_ref[0] = jnp.concatenate(outs, axis=0).T.astype(BF16)

    qts = []
    for h in range(nh):
        qt = qt_ref[0, h * HEAD_PAD:(h + 1) * HEAD_PAD, :]
        s0 = _dot(keys(h, 0), qt)
        shift = jnp.max(jnp.where(jnp.logical_or(qi > 0, causal), s0, -jnp.inf), axis=0, keepdims=True)
        p_even[h][...] = jnp.exp2(s0 - shift).astype(BF16)
        rows = lax.broadcasted_iota(jnp.int32, (BF16_ROWS, tq), 0)
        shift_rows = jnp.where(rows == 0, -shift, 0.0).astype(BF16)
        qts.append(jnp.concatenate([qt[:SHIFT_LANE], shift_rows, qt[SHIFT_LANE + BF16_ROWS:]], axis=0))
        acc_refs[h][...] = jnp.zeros((V_ROWS, tq), F32)

    def prefetch(h, blk, p_ref):
        p_ref[...] = jnp.exp2(_dot(keys(h, blk), qts[h])).astype(BF16)

    def consume(h, blk, p_ref):
        acc_refs[h][...] += _dot(values_t(h, blk), p_ref[...])

    def pair(u, carry):
        blk = per_tile * u
        for h in range(nh):
            prefetch(h, blk + 1, p_odd[h])
        for h in range(nh):
            consume(h, blk, p_even[h])
        for h in range(nh):
            prefetch(h, blk + 2, p_even[h])
        for h in range(nh):
            consume(h, blk + 1, p_odd[h])
        return carry

    def pair_group(ug, carry):
        for j in range(ATTN_PAIRS):
            carry = pair(ATTN_PAIRS * ug + j, carry)
        return carry

    n_groups = qi >> (ATTN_PAIRS.bit_length() - 1)
    lax.fori_loop(0, n_groups, pair_group, 0)
    lax.fori_loop(n_groups * ATTN_PAIRS, qi, pair, 0)

    accs = []
    for h in range(nh):
        p0 = jnp.where(causal, p_even[h][...], 0.0)
        s1 = jnp.where(causal[:, :half], _dot(keys(h, first + 1), qts[h][:, half:]), -jnp.inf)
        acc = acc_refs[h][...] + _dot(values_t(h, first), p0)
        pv1 = _dot(values_t(h, first + 1), jnp.exp2(s1).astype(BF16))
        accs.append(jnp.concatenate([acc[:, :half], acc[:, half:] + pv1], axis=1))
    write([finish(acc) for acc in accs])
    overflow = jnp.max(jnp.where(jnp.isfinite(jnp.concatenate(accs, axis=0)), 0.0, 1.0)) > 0.0

    @pl.when(overflow)
    def _():
        outs = []
        for h in range(nh):
            qt = qt_ref[0, h * HEAD_PAD:(h + 1) * HEAD_PAD, :]

            def update(s, blk, state, h=h):
                m, acc = state
                m_new = jnp.maximum(m, jnp.max(s, axis=0, keepdims=True))
                p = jnp.exp2(s - m_new).astype(BF16)
                return m_new, jnp.exp2(m - m_new) * acc + _dot(values_t(h, blk), p)

            def full_block(blk, state, h=h, qt=qt, update=update):
                return update(_dot(keys(h, blk), qt), blk, state)

            state = (jnp.full((1, tq), -jnp.inf, F32), jnp.zeros((V_ROWS, tq), F32))
            state = lax.fori_loop(0, first, full_block, state)
            for d in range(per_tile):
                s = jnp.where(lax.broadcasted_iota(jnp.int32, (tk, tq), 0) + d * tk
                              <= lax.broadcasted_iota(jnp.int32, (tk, tq), 1),
                              _dot(keys(h, first + d), qt), -jnp.inf)
                state = update(s, first + d, state)
            outs.append(finish(state[1]))
        write(outs)


def _attn(qt, k, vt, *, tq):
    b, s, hp = k.shape
    nh = ATTN_HEADS
    groups = hp // (nh * HEAD_PAD)
    assert tq == 2 * ATTN_TK, "the key loop consumes blocks in pairs"
    return pl.pallas_call(
        functools.partial(_attn_kernel, tq=tq),
        grid=(b, groups, s // tq),
        in_specs=[
            pl.BlockSpec((1, nh * HEAD_PAD, tq), lambda bi, g, i: (bi, g, i)),
            pl.BlockSpec((1, s, nh * HEAD_PAD), lambda bi, g, i: (bi, 0, g)),
            pl.BlockSpec((1, s // ATTN_TK, nh * V_ROWS, ATTN_TK), lambda bi, g, i: (bi, 0, g, 0),
                         pipeline_mode=pl.Buffered(1)),
        ],
        out_specs=pl.BlockSpec((1, tq, nh * V_HEAD), lambda bi, g, i: (bi, i, g)),
        out_shape=jax.ShapeDtypeStruct((b, s, groups * nh * V_HEAD), BF16),
        scratch_shapes=([pltpu.VMEM((V_ROWS, tq), F32)] * nh + [pltpu.VMEM((ATTN_TK, tq), BF16)] * (2 * nh)),
        compiler_params=pltpu.CompilerParams(
            dimension_semantics=("arbitrary", "arbitrary", "arbitrary"), vmem_limit_bytes=VMEM_LIMIT),
        name="attn",
    )(qt, k, vt)


def _lru_kernel(xb_ref, yb_ref, cw_ref, cb_ref, wrg_ref, brg_ref, wig_ref, big_ref, lam_ref, o_ref,
                xs, ys, outs, a_s, b_s, xhalo, hcar, *, tc):
    t = pl.program_id(1)
    nch = SUBLANES
    steps = tc // nch
    pitch = steps + SUBLANES
    n_slabs = xs.shape[0]
    lanes = [slice(c * LANES, (c + 1) * LANES) for c in range(n_slabs)]

    @pl.when(t == 0)
    def _():
        xhalo[...] = jnp.zeros(xhalo.shape, F32)
        hcar[...] = jnp.zeros(hcar.shape, F32)

    for c in range(n_slabs):
        for j in range(nch):
            data = slice(j * pitch + SUBLANES, (j + 1) * pitch)
            xs[c, data, :] = xb_ref[0, j * steps:(j + 1) * steps, lanes[c]]
            ys[c, data, :] = jax.nn.gelu(yb_ref[0, j * steps:(j + 1) * steps, lanes[c]])
            if j == 0:
                xs[c, 0:SUBLANES, :] = xhalo[c]
            else:
                xs[c, j * pitch:j * pitch + SUBLANES, :] = xb_ref[0, j * steps - SUBLANES:j * steps, lanes[c]]
        xhalo[c] = xb_ref[0, tc - SUBLANES:tc, lanes[c]]

    def chain_rows(i):
        return pl.ds(pl.multiple_of(i * nch, nch), nch)

    for c in range(n_slabs):
        taps = [jnp.broadcast_to(cw_ref[kk:kk + 1, lanes[c]], (nch, LANES)) for kk in range(CONV_WIDTH)]
        bias = jnp.broadcast_to(cb_ref[:, lanes[c]], (nch, LANES))

        def conv_step(i, carry, c=c, taps=taps, bias=bias):
            xc = bias
            for kk in range(CONV_WIDTH):
                first = SUBLANES - (CONV_WIDTH - 1) + kk + i
                xc = xc + xs[c, pl.ds(first, nch, stride=pitch), :] * taps[kk]
            b_s[c, chain_rows(i), :] = xc
            return carry

        lax.fori_loop(0, steps, conv_step, 0, unroll=LRU_UNROLL)

    half_rate = (-0.5 * LRU_C * LOG2E) * jax.nn.softplus(-lam_ref[...])
    for c in range(n_slabs):
        xg = b_s[c]
        xg16 = xg.astype(BF16)
        t_r = jnp.tanh(_dot(xg16, wrg_ref[c]) + brg_ref[:, lanes[c]])
        t_i = jnp.tanh(_dot(xg16, wig_ref[c]) + big_ref[:, lanes[c]])
        a = jnp.exp2(half_rate[:, lanes[c]] * t_r + half_rate[:, lanes[c]])
        a_s[c] = a
        b_s[c] = jnp.sqrt(1.0 - a * a) * ((0.5 * t_i + 0.5) * xg)

    def scan_step(i, carry):
        hs, decays = carry
        new_h, new_decay = [], []
        for c in range(n_slabs):
            a = a_s[c, chain_rows(i), :]
            h = a * hs[c] + b_s[c, chain_rows(i), :]
            decay = a * decays[c]
            b_s[c, chain_rows(i), :] = h
            a_s[c, chain_rows(i), :] = decay
            new_h.append(h)
            new_decay.append(decay)
        return tuple(new_h), tuple(new_decay)

    zeros = tuple(jnp.zeros((nch, LANES), F32) for _ in range(n_slabs))
    ones = tuple(jnp.ones((nch, LANES), F32) for _ in range(n_slabs))
    h_end, decay_end = lax.fori_loop(0, steps, scan_step, (zeros, ones), unroll=LRU_UNROLL)

    carry_in = []
    for c in range(n_slabs):
        cur = hcar[c, 0:1, :]
        rows = [cur]
        for j in range(1, nch):
            cur = h_end[c][j - 1:j, :] + decay_end[c][j - 1:j, :] * cur
            rows.append(cur)
        carry_in.append(jnp.concatenate(rows, axis=0))
        hcar[c] = jnp.broadcast_to(h_end[c][nch - 1:nch, :] + decay_end[c][nch - 1:nch, :] * cur, (nch, LANES))

    for c in range(n_slabs):
        def out_step(i, carry, c=c):
            h = b_s[c, chain_rows(i), :] + a_s[c, chain_rows(i), :] * carry_in[c]
            gated = pl.ds(SUBLANES + i, nch, stride=pitch)
            outs[c, gated, :] = h * ys[c, gated, :]
            return carry

        lax.fori_loop(0, steps, out_step, 0, unroll=LRU_UNROLL)
    for c in range(n_slabs):
        for j in range(nch):
            o_ref[0, j * steps:(j + 1) * steps, lanes[c]] = (
                outs[c, j * pitch + SUBLANES:(j + 1) * pitch, :].astype(BF16))


def _lru(z, cw, cb, wrg, brg, wig, big, lam, *, tc):
    b, s, _ = z.shape
    w = cw.shape[1]
    n_slabs = w // LANES
    assert n_slabs == wrg.shape[0] and tc % (SUBLANES * SUBLANES) == 0
    chain_rows = tc + SUBLANES * SUBLANES
    slab = lambda rows: pltpu.VMEM((n_slabs, rows, LANES), F32)
    return pl.pallas_call(
        functools.partial(_lru_kernel, tc=tc),
        grid=(b, s // tc),
        in_specs=[
            pl.BlockSpec((1, tc, w), lambda bi, i: (bi, i, 0)),
            pl.BlockSpec((1, tc, w), lambda bi, i: (bi, i, 1)),
            _resident(cw.shape), _resident(cb.shape), _resident(wrg.shape), _resident(brg.shape),
            _resident(wig.shape), _resident(big.shape), _resident(lam.shape),
        ],
        out_specs=pl.BlockSpec((1, tc, w), lambda bi, i: (bi, i, 0)),
        out_shape=jax.ShapeDtypeStruct((b, s, w), BF16),
        scratch_shapes=[slab(chain_rows), slab(chain_rows), slab(chain_rows), slab(tc), slab(tc),
                        slab(SUBLANES), slab(SUBLANES)],
        compiler_params=pltpu.CompilerParams(
            dimension_semantics=("arbitrary", "arbitrary"), vmem_limit_bytes=VMEM_LIMIT),
        name="lru",
    )(z, z, cw, cb, wrg, brg, wig, big, lam)


def _merge_kernel(o_ref, lg_ref, ga_ref, gb_ref, h_ref, womla_ref, wolru_ref, wout_ref, g_ref, out_ref):
    o_mla = _dot(o_ref[0], womla_ref[...])
    o_lru = _dot(lg_ref[0], wolru_ref[...])
    merged = jax.nn.sigmoid(ga_ref[0]) * o_mla + jax.nn.sigmoid(gb_ref[0]) * o_lru
    y = _dot(merged.astype(BF16), wout_ref[...])
    out_ref[0] = h_ref[0] + _rms(y, g_ref[...])


def _merge(o, lg, z, h, womla, wolru, wout, g, *, tm):
    b, s, d = h.shape
    tok = lambda col: pl.BlockSpec((1, tm, d), lambda bi, i: (bi, i, col))
    return pl.pallas_call(
        _merge_kernel,
        grid=(b, s // tm),
        in_specs=[
            pl.BlockSpec((1, tm, o.shape[2]), lambda bi, i: (bi, i, 0)),
            tok(0), tok(2), tok(3), tok(0),
            _resident(womla.shape), _resident(wolru.shape), _resident(wout.shape), _resident(g.shape),
        ],
        out_specs=tok(0),
        out_shape=jax.ShapeDtypeStruct((b, s, d), F32),
        compiler_params=pltpu.CompilerParams(
            dimension_semantics=("arbitrary", "arbitrary"), vmem_limit_bytes=VMEM_LIMIT),
        name="merge",
    )(o, lg, z, z, h, womla, wolru, wout, g)


def _pad_heads(w, per_head):
    k = w.shape[0]
    w = w.reshape(k, MLA_HEADS, per_head)
    return jnp.pad(w, ((0, 0), (0, 0), (0, HEAD_PAD - per_head))).reshape(k, MLA_HEADS * HEAD_PAD)


def _rope_consts():
    half = QK_ROPE // 2
    inv_freq = ROPE_THETA ** (-jnp.arange(half, dtype=F32) / half)
    lane = jnp.arange(HEAD_PAD)
    lo = (lane >= QK_NOPE) & (lane < QK_NOPE + half)
    hi = (lane >= QK_NOPE + half) & (lane < QK_NOPE + QK_ROPE)
    freq = jnp.where(lo | hi, inv_freq[(lane - QK_NOPE) % half], 0.0)
    rows = jnp.stack([freq, jnp.where(lo, -1.0, 0.0), jnp.where(hi, 1.0, 0.0),
                      jnp.where(lane == SHIFT_LANE, 1.0, 0.0)]).astype(F32)
    return jnp.pad(rows, ((0, SUBLANES - rows.shape[0]), (0, 0)))


def kernel(x, positions, ffn1_pre_g, ffn1_w_gate, ffn1_w_up, ffn1_w_down, ffn1_post_g, mix_pre_g, w_in, q_norm_g, w_uq, kv_norm_g, w_ukv, w_o_mla, conv_w, conv_b, w_rg, b_rg, w_ig, b_ig, lru_lambda, w_o_lru, w_out, mix_post_g, ffn2_pre_g, ffn2_w_gate, ffn2_w_up, ffn2_w_down, ffn2_post_g):
    b, s, d = x.shape
    depth = w_in.shape[0]
    tm = min(512, s)
    tm_ffn = min(1024, b * s)
    tq = 2 * ATTN_TK
    tc = min(512, s)
    ff_chunk = ffn1_w_gate.shape[2] // 2
    q_scale = (QK_NOPE + QK_ROPE) ** -0.5 * LOG2E
    n_lat = Q_LORA + KV_LORA
    rope_consts = _rope_consts()
    vones = jnp.tile((jnp.arange(V_ROWS) == V_HEAD).astype(F32), MLA_HEADS).reshape(-1, 1)
    pos_col = positions.reshape(b, s, 1)
    pos_row = positions.reshape(b, 1, s)
    half = QK_ROPE // 2
    freq_col = (ROPE_THETA ** (-jnp.arange(half, dtype=F32) / half)).reshape(half, 1)
    row = lambda v: v.reshape(1, -1)

    h = x.reshape(b * s, d)
    for l in range(depth):
        h = _ffn(h, row(ffn1_pre_g[l]), ffn1_w_gate[l].astype(BF16), ffn1_w_up[l].astype(BF16),
                 ffn1_w_down[l].astype(BF16), row(ffn1_post_g[l]), tm=tm_ffn, ff_chunk=ff_chunk)

        wi = w_in[l]
        w_krope = jnp.pad(wi[:, n_lat:n_lat + QK_ROPE], ((0, 0), (QK_NOPE, HEAD_PAD - QK_NOPE - QK_ROPE)))
        wlat = jnp.concatenate([wi[:, :n_lat], w_krope], axis=1).astype(BF16)
        wrest = wi[:, n_lat + QK_ROPE:].astype(BF16)
        wuqt = _pad_heads(w_uq[l], QK_NOPE + QK_ROPE).T.astype(BF16)
        wkv = w_ukv[l].reshape(KV_LORA, MLA_HEADS, QK_NOPE + V_HEAD)
        wk = _pad_heads(wkv[:, :, :QK_NOPE].reshape(KV_LORA, -1), QK_NOPE).astype(BF16)
        wvt = jnp.pad(wkv[:, :, QK_NOPE:], ((0, 0), (0, 0), (0, V_ROWS - V_HEAD)))
        wvt = wvt.reshape(KV_LORA, MLA_HEADS * V_ROWS).T.astype(BF16)
        h3 = h.reshape(b, s, d)
        qt, k, vt, z = _inproj(h3, pos_col, pos_row, row(mix_pre_g[l]), wlat, wrest, row(q_norm_g[l]), wuqt,
                               row(kv_norm_g[l]), wk, wvt, vones, rope_consts, freq_col,
                               tm=tm, q_scale=q_scale)

        o = _attn(qt, k, vt, tq=tq)
        lg = _lru(z, conv_w[l], row(conv_b[l]), (0.5 * w_rg[l]).astype(BF16), row(0.5 * b_rg[l]),
                  (0.5 * w_ig[l]).astype(BF16), row(0.5 * b_ig[l]), row(lru_lambda[l]), tc=tc)

        h = _merge(o, lg, z, h3, w_o_mla[l].astype(BF16), w_o_lru[l].astype(BF16), w_out[l].astype(BF16),
                   row(mix_post_g[l]), tm=tm).reshape(b * s, d)

        h = _ffn(h, row(ffn2_pre_g[l]), ffn2_w_gate[l].astype(BF16), ffn2_w_up[l].astype(BF16),
                 ffn2_w_down[l].astype(BF16), row(ffn2_post_g[l]), tm=tm_ffn, ff_chunk=ff_chunk)
    return h.reshape(b, s, d)
```
